```python
import jax, jax.numpy as jnp
from jax import lax
import numpy as np

D_MODEL = 1024
BATCH = 1
SEQ = 16384
DEPTH = 4

N_MEM = 256
HEAD_DIM = 64
FOX_HEADS = 8
RET_HEADS = 4
MEM_HEADS = 4
FOX_W = FOX_HEADS * HEAD_DIM
RET_W = RET_HEADS * HEAD_DIM
MEM_W = MEM_HEADS * HEAD_DIM
D_MIX = FOX_W + RET_W + MEM_W
Q_BLOCK = 128
RET_CHUNK = 128
ROPE_BASE = 10000.0
EPS = 1e-6
SPLIT_SIZES = [FOX_W, FOX_W, FOX_W, FOX_W, FOX_HEADS, RET_W, RET_W, RET_W, RET_W, MEM_W, MEM_W]
D_IN = sum(SPLIT_SIZES)
SPLIT_OFFSETS = [int(o) for o in np.cumsum(SPLIT_SIZES)[:-1]]

kernel_name = "hybrid_fox_retnet_memxattn_trunk"


def rmsnorm(x, g):
    xf = x.astype(jnp.float32)
    y = xf * lax.rsqrt(jnp.mean(xf * xf, axis=-1, keepdims=True) + EPS) * g.astype(jnp.float32)
    return y.astype(x.dtype)


def rotary(x, pos):
    d = x.shape[-1]
    half = d // 2
    freqs = ROPE_BASE ** (-jnp.arange(half, dtype=jnp.float32) / half)
    ang = pos[:, None] * freqs[None, :]
    cos, sin = jnp.cos(ang), jnp.sin(ang)
    xf = x.astype(jnp.float32)
    x1, x2 = xf[..., :half], xf[..., half:]
    return jnp.concatenate([x1 * cos - x2 * sin, x1 * sin + x2 * cos], axis=-1).astype(x.dtype)


def forgetting_attention(q, k, v, f_logit, b_f):
    B, H, S, d = q.shape
    log_f = jax.nn.log_sigmoid(f_logit.astype(jnp.float32) + b_f.astype(jnp.float32))
    c = jnp.cumsum(log_f, axis=1).transpose(0, 2, 1)
    nb = S // Q_BLOCK
    qb = q.reshape(B, H, nb, Q_BLOCK, d).transpose(2, 0, 1, 3, 4)
    cb = c.reshape(B, H, nb, Q_BLOCK).transpose(2, 0, 1, 3)
    posb = jnp.arange(S, dtype=jnp.int32).reshape(nb, Q_BLOCK)
    kpos = jnp.arange(S, dtype=jnp.int32)
    scale = d ** -0.5

    def block(args):
        qi, ci, pi = args
        s = jnp.einsum('bhqd,bhkd->bhqk', qi, k).astype(jnp.float32) * scale
        s = s + ci[..., None] - c[:, :, None, :]
        s = jnp.where(kpos[None, :] <= pi[:, None], s, -jnp.inf)
        p = jax.nn.softmax(s, axis=-1)
        return jnp.einsum('bhqk,bhkd->bhqd', p.astype(v.dtype), v)

    o = lax.map(block, (qb, cb, posb))
    return o.transpose(1, 2, 0, 3, 4).reshape(B, H, S, d)


def retention(q, k, v):
    B, H, S, d = q.shape
    C = RET_CHUNK
    N = S // C
    pos = jnp.arange(S, dtype=jnp.float32)
    q = rotary(q, pos)
    k = rotary(k, pos) * (d ** -0.5)
    log_gamma = jnp.log(1.0 - 2.0 ** (-5.0 - jnp.arange(H, dtype=jnp.float32)))
    idx = jnp.arange(C, dtype=jnp.float32)
    diff = idx[:, None] - idx[None, :]
    decay_intra = jnp.where(diff >= 0, jnp.exp(log_gamma[:, None, None] * jnp.maximum(diff, 0.0)), 0.0)
    zeta = jnp.exp(log_gamma[:, None] * (C - 1 - idx)[None, :])
    xi = jnp.exp(log_gamma[:, None] * (idx + 1)[None, :])
    decay_chunk = jnp.exp(log_gamma * C)[None, :, None, None]

    qc = q.reshape(B, H, N, C, d)
    kc = k.reshape(B, H, N, C, d)
    vc = v.reshape(B, H, N, C, d)
    a = jnp.einsum('bhncd,bhnmd->bhncm', qc, kc) * decay_intra[None, :, None]
    o_intra = jnp.einsum('bhncm,bhnme->bhnce', a, vc)
    kv = jnp.einsum('bhnmd,bhnme->bhnde', kc * zeta[None, :, None, :, None], vc)
    kv_seq = jnp.moveaxis(kv, 2, 0)

    def step(R, kv_i):
        return decay_chunk * R + kv_i, R

    _, R_prev = lax.scan(step, jnp.zeros_like(kv_seq[0]), kv_seq)
    R_prev = jnp.moveaxis(R_prev, 0, 2)
    o_cross = jnp.einsum('bhncd,bhnde->bhnce', qc * xi[None, :, None, :, None], R_prev)
    o = (o_intra + o_cross).reshape(B, H, S, d)
    of = o.astype(jnp.float32)
    of = of * lax.rsqrt(jnp.mean(of * of, axis=-1, keepdims=True) + EPS)
    return of.astype(v.dtype)


def memory_attention(q, mem, w_mkv, g_mem):
    B, H, S, d = q.shape
    mh = rmsnorm(mem, g_mem)
    kv = jnp.einsum('bmd,de->bme', mh, w_mkv)
    mk, mv = jnp.split(kv, 2, axis=-1)
    M = mem.shape[1]
    mk = mk.reshape(B, M, H, d)
    mv = mv.reshape(B, M, H, d)
    s = jnp.einsum('bhsd,bmhd->bhsm', q, mk).astype(jnp.float32) * (d ** -0.5)
    p = jax.nn.softmax(s, axis=-1)
    return jnp.einsum('bhsm,bmhd->bhsd', p.astype(mv.dtype), mv)


def hybrid_layer(x, mem, w_in, b_f, w_out, w_mkv, g_pre, g_post, g_mem):
    B, S, _ = x.shape
    h = rmsnorm(x, g_pre)
    p = jnp.einsum('bsd,de->bse', h, w_in)
    fq, fk, fv, fz, ff, rq, rk, rv, rz, mq, mz = jnp.split(p, SPLIT_OFFSETS, axis=-1)

    def heads(t, H):
        return t.reshape(B, S, H, HEAD_DIM).transpose(0, 2, 1, 3)

    def merge(t):
        return t.transpose(0, 2, 1, 3).reshape(B, S, -1)

    fox = forgetting_attention(heads(fq, FOX_HEADS), heads(fk, FOX_HEADS), heads(fv, FOX_HEADS), ff, b_f)
    ret = retention(heads(rq, RET_HEADS), heads(rk, RET_HEADS), heads(rv, RET_HEADS))
    mxa = memory_attention(heads(mq, MEM_HEADS), mem, w_mkv, g_mem)
    y = jnp.concatenate([
        merge(fox) * jax.nn.silu(fz),
        merge(ret) * jax.nn.silu(rz),
        merge(mxa) * jax.nn.silu(mz),
    ], axis=-1)
    o = jnp.einsum('bse,ed->bsd', y, w_out)
    return x + rmsnorm(o, g_post)


def setup_inputs(seed: int = 0) -> dict:
    key = jax.random.key(seed)
    ks = jax.random.split(key, 10)
    f32 = jnp.float32
    x = jax.random.normal(ks[0], (BATCH, SEQ, D_MODEL), f32)
    mem = jax.random.normal(ks[1], (BATCH, N_MEM, D_MODEL), f32)
    w_in = jax.random.normal(ks[2], (DEPTH, D_MODEL, D_IN), f32) * D_MODEL ** -0.5
    b_f = jnp.linspace(1.0, 5.0, FOX_HEADS, dtype=f32)[None, :] + 0.1 * jax.random.normal(ks[3], (DEPTH, FOX_HEADS), f32)
    w_out = jax.random.normal(ks[4], (DEPTH, D_MIX, D_MODEL), f32) * D_MIX ** -0.5
    w_mem_kv = jax.random.normal(ks[5], (DEPTH, D_MODEL, 2 * MEM_W), f32) * D_MODEL ** -0.5
    pre_norm = 1.0 + 0.02 * jax.random.normal(ks[6], (DEPTH, D_MODEL), f32)
    post_norm = 1.0 + 0.02 * jax.random.normal(ks[7], (DEPTH, D_MODEL), f32)
    mem_norm = 1.0 + 0.02 * jax.random.normal(ks[8], (DEPTH, D_MODEL), f32)
    return {"x": x, "mem": mem, "w_in": w_in, "b_f": b_f, "w_out": w_out,
            "w_mem_kv": w_mem_kv, "pre_norm": pre_norm, "post_norm": post_norm,
            "mem_norm": mem_norm}


def reference(x, mem, w_in, b_f, w_out, w_mem_kv, pre_norm, post_norm, mem_norm):
    for i in range(DEPTH):
        x = hybrid_layer(x, mem, w_in[i], b_f[i], w_out[i], w_mem_kv[i],
                         pre_norm[i], post_norm[i], mem_norm[i])
    return x
```

```python
import functools

import numpy as np
import jax
import jax.numpy as jnp
from jax import lax
from jax.experimental import pallas as pl
from jax.experimental.pallas import tpu as pltpu

F32 = jnp.float32
BF16 = jnp.bfloat16

D_MODEL = 1024
DEPTH = 4
N_MEM = 256
HEAD_DIM = 64
FOX_HEADS = 8
RET_HEADS = 4
MEM_HEADS = 4
FOX_W = FOX_HEADS * HEAD_DIM
RET_W = RET_HEADS * HEAD_DIM
MEM_W = MEM_HEADS * HEAD_DIM
D_MIX = FOX_W + RET_W + MEM_W
RET_CHUNK = 128
ROPE_BASE = 10000.0
EPS = 1e-6
QK_SCALE = HEAD_DIM ** -0.5

LANES = 128
NEG_BIG = -1e30

OFF_FQ = 0
OFF_FK = OFF_FQ + FOX_W
OFF_FV = OFF_FK + FOX_W
OFF_FZ = OFF_FV + FOX_W
OFF_RQ = OFF_FZ + FOX_W
OFF_RK = OFF_RQ + RET_W
OFF_RV = OFF_RK + RET_W
OFF_RZ = OFF_RV + RET_W
OFF_MQ = OFF_RZ + RET_W
OFF_MZ = OFF_MQ + MEM_W
OFF_FF = OFF_MZ + MEM_W
D_IN_PAD = OFF_FF + LANES

X_LANE = HEAD_DIM

VMEM_LIMIT = 56 * 1024 * 1024


def _split3(x):
    hi = x.astype(BF16).astype(F32)
    r1 = x - hi
    mid = r1.astype(BF16).astype(F32)
    lo = (r1 - mid).astype(BF16).astype(F32)
    return hi, mid, lo


def _log_sigmoid(x):
    return -(jnp.maximum(-x, 0.0) + jnp.log1p(jnp.exp(-jnp.abs(x))))


def _silu(z):
    return z * (1.0 / (1.0 + jnp.exp(-z)))


def _proj_kernel(x_ref, g_ref, w_ref, bf_ref, tri_ref,
                 qp_ref, kp_ref, vp_ref, fz_ref, rq_ref, rk_ref, rv_ref, rz_ref,
                 mq_ref, mz_ref, carry_ref):
    i = pl.program_id(0)

    @pl.when(i == 0)
    def _():
        carry_ref[...] = jnp.zeros_like(carry_ref)

    x = x_ref[...]
    ms = jnp.mean(x * x, axis=-1, keepdims=True)
    h = (x * lax.rsqrt(ms + EPS) * g_ref[...]).astype(BF16)

    def proj(off, width):
        return jnp.dot(h, w_ref[:, off:off + width], preferred_element_type=F32)

    tm = x.shape[0]
    lane = lax.broadcasted_iota(jnp.int32, (tm, LANES), 1)

    lf = _log_sigmoid(proj(OFF_FF, LANES) + bf_ref[...])
    tri = tri_ref[...]
    c = carry_ref[...]
    for part in _split3(lf):
        c = c + jnp.dot(tri, part.astype(BF16), preferred_element_type=F32)
    carry_ref[...] = c[tm - 1:tm, :]
    c_parts = _split3(c)

    fq = proj(OFF_FQ, FOX_W) * QK_SCALE
    fk = proj(OFF_FK, FOX_W)
    fv = proj(OFF_FV, FOX_W)
    for hd in range(FOX_HEADS):
        lo = LANES * (hd // 2)

        def head_chunk(a):
            ch = a[:, lo:lo + LANES]
            if hd % 2:
                ch = pltpu.roll(ch, HEAD_DIM, 1)
            return jnp.where(lane < HEAD_DIM, ch, 0.0)

        b0, b1, b2 = (p[:, hd:hd + 1] for p in c_parts)
        q = head_chunk(fq)
        q = jnp.where(lane == X_LANE, b0, q)
        q = jnp.where(lane == X_LANE + 1, b1, q)
        q = jnp.where(lane == X_LANE + 2, b2, q)
        q = jnp.where((lane >= X_LANE + 3) & (lane < X_LANE + 6), 1.0, q)
        k = head_chunk(fk)
        k = jnp.where((lane >= X_LANE) & (lane < X_LANE + 3), 1.0, k)
        k = jnp.where(lane == X_LANE + 3, -b0, k)
        k = jnp.where(lane == X_LANE + 4, -b1, k)
        k = jnp.where(lane == X_LANE + 5, -b2, k)
        v = jnp.where(lane == X_LANE, 1.0, head_chunk(fv))
        qp_ref[hd] = q.astype(BF16)
        kp_ref[hd] = k.astype(BF16)
        vp_ref[hd] = v.astype(BF16)

    fz_ref[...] = proj(OFF_FZ, FOX_W)
    rq_ref[...] = proj(OFF_RQ, RET_W)
    rk_ref[...] = proj(OFF_RK, RET_W)
    rv_ref[...] = proj(OFF_RV, RET_W).astype(BF16)
    rz_ref[...] = proj(OFF_RZ, RET_W)
    mq_ref[...] = (proj(OFF_MQ, MEM_W) * QK_SCALE).astype(BF16)
    mz_ref[...] = proj(OFF_MZ, MEM_W)


def _proj_call(x, g_pre, w_in_p, b_f_p, tri, tm):
    S = x.shape[0]
    row = lambda w: pl.BlockSpec((tm, w), lambda i: (i, 0))
    const = lambda shape: pl.BlockSpec(shape, lambda i: (0,) * len(shape))
    head = pl.BlockSpec((FOX_HEADS, tm, LANES), lambda i: (0, i, 0))
    out_shape = (
        jax.ShapeDtypeStruct((FOX_HEADS, S, LANES), BF16),
        jax.ShapeDtypeStruct((FOX_HEADS, S, LANES), BF16),
        jax.ShapeDtypeStruct((FOX_HEADS, S, LANES), BF16),
        jax.ShapeDtypeStruct((S, FOX_W), F32),
        jax.ShapeDtypeStruct((S, RET_W), F32),
        jax.ShapeDtypeStruct((S, RET_W), F32),
        jax.ShapeDtypeStruct((S, RET_W), BF16),
        jax.ShapeDtypeStruct((S, RET_W), F32),
        jax.ShapeDtypeStruct((S, MEM_W), BF16),
        jax.ShapeDtypeStruct((S, MEM_W), F32),
    )
    return pl.pallas_call(
        _proj_kernel,
        grid=(S // tm,),
        in_specs=[row(D_MODEL), const((1, D_MODEL)), const((D_MODEL, D_IN_PAD)),
                  const((1, LANES)), const((tm, tm))],
        out_specs=(head, head, head, row(FOX_W), row(RET_W), row(RET_W), row(RET_W),
                   row(RET_W), row(MEM_W), row(MEM_W)),
        out_shape=out_shape,
        scratch_shapes=[pltpu.VMEM((1, LANES), F32)],
        compiler_params=pltpu.CompilerParams(
            dimension_semantics=("arbitrary",), vmem_limit_bytes=VMEM_LIMIT),
        name="proj",
    )(x, g_pre, w_in_p, b_f_p, tri)


def _fox_kernel(q_ref, k_ref, v_ref, o_ref, *, bq, bk):
    i = pl.program_id(1)
    n_full = i * (bq // bk)
    lane = lax.broadcasted_iota(jnp.int32, (bq, LANES), 1)

    def tile(q, k, v, m, acc, mask):
        s = lax.dot_general(q, k, (((1,), (1,)), ((), ())), preferred_element_type=F32)
        if mask is not None:
            s = jnp.where(mask, s, NEG_BIG)
        m_new = jnp.maximum(m, jnp.max(s, axis=1, keepdims=True))
        p = jnp.exp(s - m_new)
        alpha = jnp.exp(m - m_new)
        pv = jnp.dot(p.astype(BF16), v, preferred_element_type=F32)
        return m_new, alpha * acc + pv

    outs = []
    for hh in range(2):
        q = q_ref[hh]

        def body(kb, carry):
            m, acc = carry
            start = pl.multiple_of(kb * bk, bk)
            k = k_ref[hh, pl.ds(start, bk), :]
            v = v_ref[hh, pl.ds(start, bk), :]
            return tile(q, k, v, m, acc, None)

        m0 = jnp.full((bq, 1), NEG_BIG, F32)
        acc0 = jnp.zeros((bq, LANES), F32)
        m, acc = lax.fori_loop(0, n_full, body, (m0, acc0))

        for t in range(bq // bk):
            start = pl.multiple_of((n_full + t) * bk, bk)
            k = k_ref[hh, pl.ds(start, bk), :]
            v = v_ref[hh, pl.ds(start, bk), :]
            r0 = t * bk
            rows = bq - r0
            ri = lax.broadcasted_iota(jnp.int32, (rows, bk), 0)
            ci = lax.broadcasted_iota(jnp.int32, (rows, bk), 1)
            m_t, acc_t = tile(q[r0:, :], k, v, m[r0:, :], acc[r0:, :], ci <= ri)
            if r0:
                m = jnp.concatenate([m[:r0, :], m_t], axis=0)
                acc = jnp.concatenate([acc[:r0, :], acc_t], axis=0)
            else:
                m, acc = m_t, acc_t

        outs.append(acc / acc[:, X_LANE:X_LANE + 1])

    o_ref[...] = jnp.where(lane < HEAD_DIM, outs[0], pltpu.roll(outs[1], HEAD_DIM, 1))


def _fox_call(qp, kp, vp, bq, bk):
    _, S, _ = qp.shape
    kv_spec = pl.BlockSpec((2, S, LANES), lambda p, i: (p, 0, 0), pipeline_mode=pl.Buffered(1))
    return pl.pallas_call(
        functools.partial(_fox_kernel, bq=bq, bk=bk),
        grid=(FOX_HEADS // 2, S // bq),
        in_specs=[pl.BlockSpec((2, bq, LANES), lambda p, i: (p, i, 0)), kv_spec, kv_spec],
        out_specs=pl.BlockSpec((bq, LANES), lambda p, i: (i, p)),
        out_shape=jax.ShapeDtypeStruct((S, FOX_W), F32),
        compiler_params=pltpu.CompilerParams(
            dimension_semantics=("arbitrary", "arbitrary"), vmem_limit_bytes=VMEM_LIMIT),
        name="fox",
    )(qp, kp, vp)


def _ret_kernel(q_ref, k_ref, v_ref, cos_ref, sin_ref, dintra_ref, xi_ref, zeta_ref, g_ref,
                o_ref, r_ref, *, tb):
    j = pl.program_id(1)

    @pl.when(j == 0)
    def _():
        r_ref[...] = jnp.zeros_like(r_ref)

    C = RET_CHUNK
    half = HEAD_DIM // 2
    lane = lax.broadcasted_iota(jnp.int32, (tb, LANES), 1)
    first_half = (lane % HEAD_DIM) < half
    cos = cos_ref[...]
    sin = sin_ref[...]

    def rotary(x):
        swapped = jnp.where(first_half, pltpu.roll(x, LANES - half, 1), pltpu.roll(x, half, 1))
        return x * cos + swapped * sin

    q = rotary(q_ref[...])
    k = rotary(k_ref[...]) * QK_SCALE
    v = v_ref[...]
    xi = xi_ref[0]
    zeta = zeta_ref[0]
    g = g_ref[0]
    lane_c = lax.broadcasted_iota(jnp.int32, (C, LANES), 1)
    head0 = lane_c < HEAD_DIM
    same_head = (lax.broadcasted_iota(jnp.int32, (LANES, LANES), 0) // HEAD_DIM
                 == lax.broadcasted_iota(jnp.int32, (LANES, LANES), 1) // HEAD_DIM)
    nt = (((1,), (1,)), ((), ()))
    tn = (((0,), (0,)), ((), ()))

    R = r_ref[...]
    for n in range(tb // C):
        qc = q[n * C:(n + 1) * C, :]
        kc = k[n * C:(n + 1) * C, :]
        vc = v[n * C:(n + 1) * C, :]
        kcb = kc.astype(BF16)
        o_intra = []
        for hh in range(2):
            qm = jnp.where(head0 if hh == 0 else ~head0, qc, 0.0).astype(BF16)
            a = lax.dot_general(qm, kcb, nt, preferred_element_type=F32) * dintra_ref[0, hh]
            o_intra.append(jnp.dot(a.astype(BF16), vc, preferred_element_type=F32))
        o = jnp.where(head0, o_intra[0], o_intra[1])
        o = o + jnp.dot((qc * xi).astype(BF16), R.astype(BF16), preferred_element_type=F32)
        kv = lax.dot_general((kc * zeta).astype(BF16), vc, tn, preferred_element_type=F32)
        R = g * R + jnp.where(same_head, kv, 0.0)
        sq = o * o
        s0 = jnp.sum(jnp.where(head0, sq, 0.0), axis=1, keepdims=True)
        s1 = jnp.sum(jnp.where(head0, 0.0, sq), axis=1, keepdims=True)
        ms = jnp.where(head0, s0, s1) * (1.0 / HEAD_DIM)
        o_ref[n * C:(n + 1) * C, :] = o * lax.rsqrt(ms + EPS)
    r_ref[...] = R


def _ret_call(rq, rk, rv, tabs, tb):
    S = rq.shape[0]
    cos, sin, dintra, xi, zeta, g = tabs
    row = pl.BlockSpec((tb, LANES), lambda p, j: (j, p))
    pair = lambda shape: pl.BlockSpec((1,) + shape, lambda p, j: (p,) + (0,) * len(shape))
    return pl.pallas_call(
        functools.partial(_ret_kernel, tb=tb),
        grid=(RET_HEADS // 2, S // tb),
        in_specs=[row, row, row, row, row,
                  pair((2, RET_CHUNK, RET_CHUNK)), pair((RET_CHUNK, LANES)),
                  pair((RET_CHUNK, LANES)), pair((LANES, LANES))],
        out_specs=row,
        out_shape=jax.ShapeDtypeStruct((S, RET_W), F32),
        scratch_shapes=[pltpu.VMEM((LANES, LANES), F32)],
        compiler_params=pltpu.CompilerParams(
            dimension_semantics=("arbitrary", "arbitrary"), vmem_limit_bytes=VMEM_LIMIT),
        name="ret",
    )(rq, rk, rv, cos, sin, dintra, xi, zeta, g)


def _ret_tables(S):
    half = HEAD_DIM // 2
    C = RET_CHUNK
    pos = jnp.arange(S, dtype=F32)
    freqs = ROPE_BASE ** (-jnp.arange(half, dtype=F32) / half)
    ang = pos[:, None] * freqs[None, :]
    cos, sin = jnp.cos(ang), jnp.sin(ang)
    cos_t = jnp.tile(jnp.concatenate([cos, cos], axis=1), (1, RET_HEADS))
    sin_t = jnp.tile(jnp.concatenate([-sin, sin], axis=1), (1, RET_HEADS))
    log_gamma = jnp.log(1.0 - 2.0 ** (-5.0 - jnp.arange(RET_HEADS, dtype=F32)))
    idx = jnp.arange(C, dtype=F32)
    diff = idx[:, None] - idx[None, :]
    decay_intra = jnp.where(diff >= 0, jnp.exp(log_gamma[:, None, None] * jnp.maximum(diff, 0.0)), 0.0)
    zeta = jnp.exp(log_gamma[:, None] * (C - 1 - idx)[None, :])
    xi = jnp.exp(log_gamma[:, None] * (idx + 1)[None, :])
    decay_chunk = jnp.exp(log_gamma * C)
    npair = RET_HEADS // 2
    dintra = decay_intra.reshape(npair, 2, C, C)
    lanes = lambda t: jnp.repeat(t.reshape(npair, 2, C).transpose(0, 2, 1), HEAD_DIM, axis=2)
    gd = jnp.repeat(decay_chunk.reshape(npair, 2), HEAD_DIM, axis=1)
    blk = jnp.arange(LANES) // HEAD_DIM
    g = jnp.where(blk[:, None] == blk[None, :], gd[:, :, None], 0.0)
    return cos_t, sin_t, dintra, lanes(xi), lanes(zeta), g


def _memkv_kernel(mem_ref, g_ref, w_ref, mk_ref, mv_ref):
    x = mem_ref[...]
    ms = jnp.mean(x * x, axis=-1, keepdims=True)
    h = (x * lax.rsqrt(ms + EPS) * g_ref[0]).astype(BF16)
    kv = jnp.dot(h, w_ref[0], preferred_element_type=F32)
    mk_ref[0] = kv[:, :MEM_W].astype(BF16)
    mv_ref[0] = kv[:, MEM_W:].astype(BF16)


def _memkv_call(mem, g_mem, w_mkv):
    M = mem.shape[0]
    return pl.pallas_call(
        _memkv_kernel,
        grid=(DEPTH,),
        in_specs=[pl.BlockSpec((M, D_MODEL), lambda l: (0, 0)),
                  pl.BlockSpec((1, 1, D_MODEL), lambda l: (l, 0, 0)),
                  pl.BlockSpec((1, D_MODEL, 2 * MEM_W), lambda l: (l, 0, 0))],
        out_specs=(pl.BlockSpec((1, M, MEM_W), lambda l: (l, 0, 0)),
                   pl.BlockSpec((1, M, MEM_W), lambda l: (l, 0, 0))),
        out_shape=(jax.ShapeDtypeStruct((DEPTH, M, MEM_W), BF16),
                   jax.ShapeDtypeStruct((DEPTH, M, MEM_W), BF16)),
        compiler_params=pltpu.CompilerParams(
            dimension_semantics=("arbitrary",), vmem_limit_bytes=VMEM_LIMIT),
        name="memkv",
    )(mem, g_mem, w_mkv)


def _out_kernel(x_ref, fox_ref, fz_ref, ret_ref, rz_ref, mq_ref, mz_ref, mk_ref, mv_ref,
                w_ref, g_ref, o_ref):
    tm = x_ref.shape[0]
    lane = lax.broadcasted_iota(jnp.int32, (tm, LANES), 1)
    head0 = lane < HEAD_DIM
    nt = (((1,), (1,)), ((), ()))

    mxa = []
    for pr in range(MEM_HEADS // 2):
        cols = slice(pr * LANES, (pr + 1) * LANES)
        q2 = mq_ref[:, cols]
        mk2 = mk_ref[0, :, cols]
        mv2 = mv_ref[0, :, cols]
        o_h = []
        for hh in range(2):
            qm = jnp.where(head0 if hh == 0 else ~head0, q2, jnp.zeros_like(q2))
            s = lax.dot_general(qm, mk2, nt, preferred_element_type=F32)
            e = jnp.exp(s - jnp.max(s, axis=1, keepdims=True))
            p = e / jnp.sum(e, axis=1, keepdims=True)
            o_h.append(jnp.dot(p.astype(BF16), mv2, preferred_element_type=F32))
        mxa.append(jnp.where(head0, o_h[0], o_h[1]))
    mxa = jnp.concatenate(mxa, axis=1)

    y = jnp.concatenate([
        (fox_ref[...] * _silu(fz_ref[...])).astype(BF16),
        (ret_ref[...] * _silu(rz_ref[...])).astype(BF16),
        (mxa * _silu(mz_ref[...])).astype(BF16),
    ], axis=1)
    o = jnp.dot(y, w_ref[...], preferred_element_type=F32)
    ms = jnp.mean(o * o, axis=-1, keepdims=True)
    o_ref[...] = x_ref[...] + o * lax.rsqrt(ms + EPS) * g_ref[...]


def _out_call(x, fox, fz, ret, rz, mq, mz, mk, mv, layer, w_out, g_post, tm):
    S = x.shape[0]
    M = mk.shape[1]
    row = lambda w: pl.BlockSpec((tm, w), lambda i: (i, 0))
    const = lambda shape: pl.BlockSpec(shape, lambda i: (0,) * len(shape))
    mem_spec = pl.BlockSpec((1, M, MEM_W), lambda i: (layer, 0, 0))
    return pl.pallas_call(
        _out_kernel,
        grid=(S // tm,),
        in_specs=[row(D_MODEL), row(FOX_W), row(FOX_W), row(RET_W), row(RET_W), row(MEM_W),
                  row(MEM_W), mem_spec, mem_spec, const((D_MIX, D_MODEL)), const((1, D_MODEL))],
        out_specs=row(D_MODEL),
        out_shape=jax.ShapeDtypeStruct((S, D_MODEL), F32),
        compiler_params=pltpu.CompilerParams(
            dimension_semantics=("arbitrary",), vmem_limit_bytes=VMEM_LIMIT),
        name="out",
    )(x, fox, fz, ret, rz, mq, mz, mk, mv, w_out, g_post)


def _tiles(S):
    tm = min(512, S)
    bq = min(512, S)
    bk = min(256, S)
    tb = min(1024, S)
    return tm, bq, bk, tb


def _reorder_w_in(w_in):
    ff0 = 4 * FOX_W
    main = jnp.concatenate([w_in[:, :, :ff0], w_in[:, :, ff0 + FOX_HEADS:]], axis=2)
    ff = w_in[:, :, ff0:ff0 + FOX_HEADS]
    pad = jnp.zeros(ff.shape[:2] + (LANES - FOX_HEADS,), w_in.dtype)
    return jnp.concatenate([main, ff, pad], axis=2).astype(BF16)


def kernel(x, mem, w_in, b_f, w_out, w_mem_kv, pre_norm, post_norm, mem_norm):
    B, S, _ = x.shape
    assert B == 1
    tm, bq, bk, tb = _tiles(S)
    w_in_p = _reorder_w_in(w_in)
    b_f_p = jnp.pad(b_f, ((0, 0), (0, LANES - FOX_HEADS)))
    tri = jnp.tril(jnp.ones((tm, tm), F32)).astype(BF16)
    tabs = _ret_tables(S)
    mk, mv = _memkv_call(mem[0], mem_norm[:, None, :], w_mem_kv.astype(BF16))
    w_out_b = w_out.astype(BF16)

    xs = x[0]
    for l in range(DEPTH):
        qp, kp, vp, fz, rq, rk, rv, rz, mq, mz = _proj_call(
            xs, pre_norm[l][None, :], w_in_p[l], b_f_p[l][None, :], tri, tm)
        fox = _fox_call(qp, kp, vp, bq, bk)
        ret = _ret_call(rq, rk, rv, tabs, tb)
        xs = _out_call(xs, fox, fz, ret, rz, mq, mz, mk, mv, l, w_out_b[l],
                       post_norm[l][None, :], tm)
    return xs[None]
```

```python
import functools

import numpy as np
import jax
import jax.numpy as jnp
from jax import lax
from jax.experimental import pallas as pl
from jax.experimental.pallas import tpu as pltpu

F32 = jnp.float32
BF16 = jnp.bfloat16

D_MODEL = 1024
DEPTH = 4
N_MEM = 256
HEAD_DIM = 64
FOX_HEADS = 8
RET_HEADS = 4
MEM_HEADS = 4
FOX_W = FOX_HEADS * HEAD_DIM
RET_W = RET_HEADS * HEAD_DIM
MEM_W = MEM_HEADS * HEAD_DIM
D_MIX = FOX_W + RET_W + MEM_W
RET_CHUNK = 128
ROPE_BASE = 10000.0
EPS = 1e-6
QK_SCALE = HEAD_DIM ** -0.5

LANES = 128
NEG_BIG = -1e30

OFF_FQ = 0
OFF_FK = OFF_FQ + FOX_W
OFF_FV = OFF_FK + FOX_W
OFF_FZ = OFF_FV + FOX_W
OFF_RQ = OFF_FZ + FOX_W
OFF_RK = OFF_RQ + RET_W
OFF_RV = OFF_RK + RET_W
OFF_RZ = OFF_RV + RET_W
OFF_MQ = OFF_RZ + RET_W
OFF_MZ = OFF_MQ + MEM_W
OFF_FF = OFF_MZ + MEM_W
D_IN_PAD = OFF_FF + LANES

X_LANE = HEAD_DIM

VMEM_LIMIT = 56 * 1024 * 1024


def _split3(x):
    hi = x.astype(BF16).astype(F32)
    r1 = x - hi
    mid = r1.astype(BF16).astype(F32)
    lo = (r1 - mid).astype(BF16).astype(F32)
    return hi, mid, lo


def _log_sigmoid(x):
    return -(jnp.maximum(-x, 0.0) + jnp.log1p(jnp.exp(-jnp.abs(x))))


def _silu(z):
    return z * (1.0 / (1.0 + jnp.exp(-z)))


def _proj_kernel(x_ref, g_ref, w_ref, bf_ref, tri_ref,
                 qp_ref, kp_ref, vp_ref, fz_ref, rq_ref, rk_ref, rv_ref, rz_ref,
                 mq_ref, mz_ref, carry_ref):
    i = pl.program_id(0)

    @pl.when(i == 0)
    def _():
        carry_ref[...] = jnp.zeros_like(carry_ref)

    x = x_ref[...]
    ms = jnp.mean(x * x, axis=-1, keepdims=True)
    h = (x * lax.rsqrt(ms + EPS) * g_ref[...]).astype(BF16)

    def proj(off, width):
        return jnp.dot(h, w_ref[:, off:off + width], preferred_element_type=F32)

    tm = x.shape[0]
    lane = lax.broadcasted_iota(jnp.int32, (tm, LANES), 1)

    lf = _log_sigmoid(proj(OFF_FF, LANES) + bf_ref[...])
    tri = tri_ref[...]
    c = carry_ref[...]
    for part in _split3(lf):
        c = c + jnp.dot(tri, part.astype(BF16), preferred_element_type=F32)
    carry_ref[...] = c[tm - 1:tm, :]
    c_parts = _split3(c)

    fq = proj(OFF_FQ, FOX_W) * QK_SCALE
    fk = proj(OFF_FK, FOX_W)
    fv = proj(OFF_FV, FOX_W)
    for hd in range(FOX_HEADS):
        lo = LANES * (hd // 2)

        def head_chunk(a):
            ch = a[:, lo:lo + LANES]
            if hd % 2:
                ch = pltpu.roll(ch, HEAD_DIM, 1)
            return jnp.where(lane < HEAD_DIM, ch, 0.0)

        b0, b1, b2 = (p[:, hd:hd + 1] for p in c_parts)
        q = head_chunk(fq)
        q = jnp.where(lane == X_LANE, b0, q)
        q = jnp.where(lane == X_LANE + 1, b1, q)
        q = jnp.where(lane == X_LANE + 2, b2, q)
        q = jnp.where((lane >= X_LANE + 3) & (lane < X_LANE + 6), 1.0, q)
        k = head_chunk(fk)
        k = jnp.where((lane >= X_LANE) & (lane < X_LANE + 3), 1.0, k)
        k = jnp.where(lane == X_LANE + 3, -b0, k)
        k = jnp.where(lane == X_LANE + 4, -b1, k)
        k = jnp.where(lane == X_LANE + 5, -b2, k)
        v = jnp.where(lane == X_LANE, 1.0, head_chunk(fv))
        qp_ref[hd] = q.T.astype(BF16)
        kp_ref[hd] = k.astype(BF16)
        vp_ref[hd] = v.T.astype(BF16)

    fz_ref[...] = proj(OFF_FZ, FOX_W)
    rq_ref[...] = proj(OFF_RQ, RET_W)
    rk_ref[...] = proj(OFF_RK, RET_W)
    rv_ref[...] = proj(OFF_RV, RET_W).astype(BF16)
    rz_ref[...] = proj(OFF_RZ, RET_W)
    mq_ref[...] = (proj(OFF_MQ, MEM_W) * QK_SCALE).astype(BF16)
    mz_ref[...] = proj(OFF_MZ, MEM_W)


def _proj_call(x, g_pre, w_in_p, b_f_p, tri, tm):
    S = x.shape[0]
    row = lambda w: pl.BlockSpec((tm, w), lambda i: (i, 0))
    const = lambda shape: pl.BlockSpec(shape, lambda i: (0,) * len(shape))
    head = pl.BlockSpec((FOX_HEADS, tm, LANES), lambda i: (0, i, 0))
    head_t = pl.BlockSpec((FOX_HEADS, LANES, tm), lambda i: (0, 0, i))
    out_shape = (
        jax.ShapeDtypeStruct((FOX_HEADS, LANES, S), BF16),
        jax.ShapeDtypeStruct((FOX_HEADS, S, LANES), BF16),
        jax.ShapeDtypeStruct((FOX_HEADS, LANES, S), BF16),
        jax.ShapeDtypeStruct((S, FOX_W), F32),
        jax.ShapeDtypeStruct((S, RET_W), F32),
        jax.ShapeDtypeStruct((S, RET_W), F32),
        jax.ShapeDtypeStruct((S, RET_W), BF16),
        jax.ShapeDtypeStruct((S, RET_W), F32),
        jax.ShapeDtypeStruct((S, MEM_W), BF16),
        jax.ShapeDtypeStruct((S, MEM_W), F32),
    )
    return pl.pallas_call(
        _proj_kernel,
        grid=(S // tm,),
        in_specs=[row(D_MODEL), const((1, D_MODEL)), const((D_MODEL, D_IN_PAD)),
                  const((1, LANES)), const((tm, tm))],
        out_specs=(head_t, head, head_t, row(FOX_W), row(RET_W), row(RET_W), row(RET_W),
                   row(RET_W), row(MEM_W), row(MEM_W)),
        out_shape=out_shape,
        scratch_shapes=[pltpu.VMEM((1, LANES), F32)],
        compiler_params=pltpu.CompilerParams(
            dimension_semantics=("arbitrary",), vmem_limit_bytes=VMEM_LIMIT),
        name="proj",
    )(x, g_pre, w_in_p, b_f_p, tri)


def _max_rows(s):
    rows = s.shape[0]
    while rows > 8:
        rows //= 2
        s = jnp.maximum(s[:rows, :], s[rows:, :])
    return jnp.max(s, axis=0, keepdims=True)


def _fox_kernel(qt_ref, k_ref, vt_ref, o_ref, m_ref, acc_ref, sa_ref, sb_ref, *, bq, bk, sub):
    assert bk == bq
    i = pl.program_id(1)
    nsub = bq // sub
    chains = [(hh, sb) for hh in range(2) for sb in range(nsub)]

    m_ref[...] = jnp.full(m_ref.shape, NEG_BIG, F32)
    acc_ref[...] = jnp.zeros(acc_ref.shape, F32)

    def scores(c, kb):
        hh, sb = c
        k = k_ref[hh, pl.ds(pl.multiple_of(kb * bk, bk), bk), :]
        return jnp.dot(k, qt_ref[hh, :, sb * sub:(sb + 1) * sub], preferred_element_type=F32)

    def finish(c, s, kb, masked):
        hh, sb = c
        cols = slice(sb * sub, (sb + 1) * sub)
        if masked:
            key = lax.broadcasted_iota(jnp.int32, (bk, sub), 0)
            qry = lax.broadcasted_iota(jnp.int32, (bk, sub), 1) + sb * sub
            s = jnp.where(key <= qry, s, NEG_BIG)
        vt = vt_ref[hh, :, pl.ds(pl.multiple_of(kb * bk, bk), bk)]
        m = m_ref[hh, :, cols]
        m_new = jnp.maximum(m, _max_rows(s))
        p = jnp.exp(s - m_new)
        alpha = jnp.exp(m - m_new)
        pv = jnp.dot(vt, p.astype(BF16), preferred_element_type=F32)
        acc_ref[hh, :, cols] = alpha * acc_ref[hh, :, cols] + pv
        m_ref[hh, :, cols] = m_new

    def step(kb, cur, nxt, masked=False, last=False):
        for n, c in enumerate(chains):
            if not last:
                nxt[n] = scores(c, kb + 1)
            finish(c, cur[n], kb, masked)

    for n, c in enumerate(chains):
        sa_ref[n] = scores(c, 0)

    def body(j, carry):
        step(2 * j, sa_ref, sb_ref)
        step(2 * j + 1, sb_ref, sa_ref)
        return carry

    lax.fori_loop(0, i // 2, body, 0)

    @pl.when(i % 2 == 0)
    def _():
        step(i, sa_ref, None, masked=True, last=True)

    @pl.when(i % 2 == 1)
    def _():
        step(i - 1, sa_ref, sb_ref)
        step(i, sb_ref, None, masked=True, last=True)

    lane = lax.broadcasted_iota(jnp.int32, (bq, LANES), 1)
    outs = []
    for hh in range(2):
        o = acc_ref[hh].T
        outs.append(o / o[:, X_LANE:X_LANE + 1])
    o_ref[...] = jnp.where(lane < HEAD_DIM, outs[0], pltpu.roll(outs[1], HEAD_DIM, 1))


def _fox_call(qt, kp, vt, bq, bk, sub):
    _, S, _ = kp.shape
    k_spec = pl.BlockSpec((2, S, LANES), lambda p, i: (p, 0, 0), pipeline_mode=pl.Buffered(1))
    vt_spec = pl.BlockSpec((2, LANES, S), lambda p, i: (p, 0, 0), pipeline_mode=pl.Buffered(1))
    return pl.pallas_call(
        functools.partial(_fox_kernel, bq=bq, bk=bk, sub=sub),
        grid=(FOX_HEADS // 2, S // bq),
        in_specs=[pl.BlockSpec((2, LANES, bq), lambda p, i: (p, 0, i)), k_spec, vt_spec],
        out_specs=pl.BlockSpec((bq, LANES), lambda p, i: (i, p)),
        out_shape=jax.ShapeDtypeStruct((S, FOX_W), F32),
        scratch_shapes=[pltpu.VMEM((2, 1, bq), F32), pltpu.VMEM((2, LANES, bq), F32),
                        pltpu.VMEM((2 * bq // sub, bk, sub), F32),
                        pltpu.VMEM((2 * bq // sub, bk, sub), F32)],
        compiler_params=pltpu.CompilerParams(
            dimension_semantics=("arbitrary", "arbitrary"), vmem_limit_bytes=VMEM_LIMIT),
        name="fox",
    )(qt, kp, vt)


def _ret_kernel(q_ref, k_ref, v_ref, cos_ref, sin_ref, dintra_ref, xi_ref, zeta_ref, g_ref,
                o_ref, r_ref, *, tb):
    j = pl.program_id(1)

    @pl.when(j == 0)
    def _():
        r_ref[...] = jnp.zeros_like(r_ref)

    C = RET_CHUNK
    half = HEAD_DIM // 2
    lane = lax.broadcasted_iota(jnp.int32, (tb, LANES), 1)
    first_half = (lane % HEAD_DIM) < half
    cos = cos_ref[...]
    sin = sin_ref[...]

    def rotary(x):
        swapped = jnp.where(first_half, pltpu.roll(x, LANES - half, 1), pltpu.roll(x, half, 1))
        return x * cos + swapped * sin

    q = rotary(q_ref[...])
    k = rotary(k_ref[...]) * QK_SCALE
    v = v_ref[...]
    xi = xi_ref[0]
    zeta = zeta_ref[0]
    g = g_ref[0]
    lane_c = lax.broadcasted_iota(jnp.int32, (C, LANES), 1)
    head0 = lane_c < HEAD_DIM
    same_head = (lax.broadcasted_iota(jnp.int32, (LANES, LANES), 0) // HEAD_DIM
                 == lax.broadcasted_iota(jnp.int32, (LANES, LANES), 1) // HEAD_DIM)
    nt = (((1,), (1,)), ((), ()))
    tn = (((0,), (0,)), ((), ()))

    R = r_ref[...]
    for n in range(tb // C):
        qc = q[n * C:(n + 1) * C, :]
        kc = k[n * C:(n + 1) * C, :]
        vc = v[n * C:(n + 1) * C, :]
        kcb = kc.astype(BF16)
        o_intra = []
        for hh in range(2):
            qm = jnp.where(head0 if hh == 0 else ~head0, qc, 0.0).astype(BF16)
            a = lax.dot_general(qm, kcb, nt, preferred_element_type=F32) * dintra_ref[0, hh]
            o_intra.append(jnp.dot(a.astype(BF16), vc, preferred_element_type=F32))
        o = jnp.where(head0, o_intra[0], o_intra[1])
        o = o + jnp.dot((qc * xi).astype(BF16), R.astype(BF16), preferred_element_type=F32)
        kv = lax.dot_general((kc * zeta).astype(BF16), vc, tn, preferred_element_type=F32)
        R = g * R + jnp.where(same_head, kv, 0.0)
        sq = o * o
        s0 = jnp.sum(jnp.where(head0, sq, 0.0), axis=1, keepdims=True)
        s1 = jnp.sum(jnp.where(head0, 0.0, sq), axis=1, keepdims=True)
        ms = jnp.where(head0, s0, s1) * (1.0 / HEAD_DIM)
        o_ref[n * C:(n + 1) * C, :] = o * lax.rsqrt(ms + EPS)
    r_ref[...] = R


def _ret_call(rq, rk, rv, tabs, tb):
    S = rq.shape[0]
    cos, sin, dintra, xi, zeta, g = tabs
    row = pl.BlockSpec((tb, LANES), lambda p, j: (j, p))
    pair = lambda shape: pl.BlockSpec((1,) + shape, lambda p, j: (p,) + (0,) * len(shape))
    return pl.pallas_call(
        functools.partial(_ret_kernel, tb=tb),
        grid=(RET_HEADS // 2, S // tb),
        in_specs=[row, row, row, row, row,
                  pair((2, RET_CHUNK, RET_CHUNK)), pair((RET_CHUNK, LANES)),
                  pair((RET_CHUNK, LANES)), pair((LANES, LANES))],
        out_specs=row,
        out_shape=jax.ShapeDtypeStruct((S, RET_W), F32),
        scratch_shapes=[pltpu.VMEM((LANES, LANES), F32)],
        compiler_params=pltpu.CompilerParams(
            dimension_semantics=("arbitrary", "arbitrary"), vmem_limit_bytes=VMEM_LIMIT),
        name="ret",
    )(rq, rk, rv, cos, sin, dintra, xi, zeta, g)


def _ret_tables(S):
    half = HEAD_DIM // 2
    C = RET_CHUNK
    pos = jnp.arange(S, dtype=F32)
    freqs = ROPE_BASE ** (-jnp.arange(half, dtype=F32) / half)
    ang = pos[:, None] * freqs[None, :]
    cos, sin = jnp.cos(ang), jnp.sin(ang)
    cos_t = jnp.tile(jnp.concatenate([cos, cos], axis=1), (1, RET_HEADS))
    sin_t = jnp.tile(jnp.concatenate([-sin, sin], axis=1), (1, RET_HEADS))
    log_gamma = jnp.log(1.0 - 2.0 ** (-5.0 - jnp.arange(RET_HEADS, dtype=F32)))
    idx = jnp.arange(C, dtype=F32)
    diff = idx[:, None] - idx[None, :]
    decay_intra = jnp.where(diff >= 0, jnp.exp(log_gamma[:, None, None] * jnp.maximum(diff, 0.0)), 0.0)
    zeta = jnp.exp(log_gamma[:, None] * (C - 1 - idx)[None, :])
    xi = jnp.exp(log_gamma[:, None] * (idx + 1)[None, :])
    decay_chunk = jnp.exp(log_gamma * C)
    npair = RET_HEADS // 2
    dintra = decay_intra.reshape(npair, 2, C, C)
    lanes = lambda t: jnp.repeat(t.reshape(npair, 2, C).transpose(0, 2, 1), HEAD_DIM, axis=2)
    gd = jnp.repeat(decay_chunk.reshape(npair, 2), HEAD_DIM, axis=1)
    blk = jnp.arange(LANES) // HEAD_DIM
    g = jnp.where(blk[:, None] == blk[None, :], gd[:, :, None], 0.0)
    return cos_t, sin_t, dintra, lanes(xi), lanes(zeta), g


def _memkv_kernel(mem_ref, g_ref, w_ref, mk_ref, mv_ref):
    x = mem_ref[...]
    ms = jnp.mean(x * x, axis=-1, keepdims=True)
    h = (x * lax.rsqrt(ms + EPS) * g_ref[0]).astype(BF16)
    kv = jnp.dot(h, w_ref[0], preferred_element_type=F32)
    mk_ref[0] = kv[:, :MEM_W].astype(BF16)
    mv_ref[0] = kv[:, MEM_W:].astype(BF16)


def _memkv_call(mem, g_mem, w_mkv):
    M = mem.shape[0]
    return pl.pallas_call(
        _memkv_kernel,
        grid=(DEPTH,),
        in_specs=[pl.BlockSpec((M, D_MODEL), lambda l: (0, 0)),
                  pl.BlockSpec((1, 1, D_MODEL), lambda l: (l, 0, 0)),
                  pl.BlockSpec((1, D_MODEL, 2 * MEM_W), lambda l: (l, 0, 0))],
        out_specs=(pl.BlockSpec((1, M, MEM_W), lambda l: (l, 0, 0)),
                   pl.BlockSpec((1, M, MEM_W), lambda l: (l, 0, 0))),
        out_shape=(jax.ShapeDtypeStruct((DEPTH, M, MEM_W), BF16),
                   jax.ShapeDtypeStruct((DEPTH, M, MEM_W), BF16)),
        compiler_params=pltpu.CompilerParams(
            dimension_semantics=("arbitrary",), vmem_limit_bytes=VMEM_LIMIT),
        name="memkv",
    )(mem, g_mem, w_mkv)


def _out_kernel(x_ref, fox_ref, fz_ref, ret_ref, rz_ref, mq_ref, mz_ref, mk_ref, mv_ref,
                w_ref, g_ref, o_ref):
    tm = x_ref.shape[0]
    lane = lax.broadcasted_iota(jnp.int32, (tm, LANES), 1)
    head0 = lane < HEAD_DIM
    nt = (((1,), (1,)), ((), ()))

    mxa = []
    for pr in range(MEM_HEADS // 2):
        cols = slice(pr * LANES, (pr + 1) * LANES)
        q2 = mq_ref[:, cols]
        mk2 = mk_ref[0, :, cols]
        mv2 = mv_ref[0, :, cols]
        o_h = []
        for hh in range(2):
            qm = jnp.where(head0 if hh == 0 else ~head0, q2, jnp.zeros_like(q2))
            s = lax.dot_general(qm, mk2, nt, preferred_element_type=F32)
            e = jnp.exp(s - jnp.max(s, axis=1, keepdims=True))
            p = e / jnp.sum(e, axis=1, keepdims=True)
            o_h.append(jnp.dot(p.astype(BF16), mv2, preferred_element_type=F32))
        mxa.append(jnp.where(head0, o_h[0], o_h[1]))
    mxa = jnp.concatenate(mxa, axis=1)

    y = jnp.concatenate([
        (fox_ref[...] * _silu(fz_ref[...])).astype(BF16),
        (ret_ref[...] * _silu(rz_ref[...])).astype(BF16),
        (mxa * _silu(mz_ref[...])).astype(BF16),
    ], axis=1)
    o = jnp.dot(y, w_ref[...], preferred_element_type=F32)
    ms = jnp.mean(o * o, axis=-1, keepdims=True)
    o_ref[...] = x_ref[...] + o * lax.rsqrt(ms + EPS) * g_ref[...]


def _out_call(x, fox, fz, ret, rz, mq, mz, mk, mv, layer, w_out, g_post, tm):
    S = x.shape[0]
    M = mk.shape[1]
    row = lambda w: pl.BlockSpec((tm, w), lambda i: (i, 0))
    const = lambda shape: pl.BlockSpec(shape, lambda i: (0,) * len(shape))
    mem_spec = pl.BlockSpec((1, M, MEM_W), lambda i: (layer, 0, 0))
    return pl.pallas_call(
        _out_kernel,
        grid=(S // tm,),
        in_specs=[row(D_MODEL), row(FOX_W), row(FOX_W), row(RET_W), row(RET_W), row(MEM_W),
                  row(MEM_W), mem_spec, mem_spec, const((D_MIX, D_MODEL)), const((1, D_MODEL))],
        out_specs=row(D_MODEL),
        out_shape=jax.ShapeDtypeStruct((S, D_MODEL), F32),
        compiler_params=pltpu.CompilerParams(
            dimension_semantics=("arbitrary",), vmem_limit_bytes=VMEM_LIMIT),
        name="out",
    )(x, fox, fz, ret, rz, mq, mz, mk, mv, w_out, g_post)


def _tiles(S):
    tm = min(512, S)
    bq = min(512, S)
    bk = min(512, S)
    sub = min(256, bq)
    tb = min(1024, S)
    return tm, bq, bk, sub, tb


def _reorder_w_in(w_in):
    ff0 = 4 * FOX_W
    main = jnp.concatenate([w_in[:, :, :ff0], w_in[:, :, ff0 + FOX_HEADS:]], axis=2)
    ff = w_in[:, :, ff0:ff0 + FOX_HEADS]
    pad = jnp.zeros(ff.shape[:2] + (LANES - FOX_HEADS,), w_in.dtype)
    return jnp.concatenate([main, ff, pad], axis=2).astype(BF16)


def kernel(x, mem, w_in, b_f, w_out, w_mem_kv, pre_norm, post_norm, mem_norm):
    B, S, _ = x.shape
    assert B == 1
    tm, bq, bk, sub, tb = _tiles(S)
    w_in_p = _reorder_w_in(w_in)
    b_f_p = jnp.pad(b_f, ((0, 0), (0, LANES - FOX_HEADS)))
    tri = jnp.tril(jnp.ones((tm, tm), F32)).astype(BF16)
    tabs = _ret_tables(S)
    mk, mv = _memkv_call(mem[0], mem_norm[:, None, :], w_mem_kv.astype(BF16))
    w_out_b = w_out.astype(BF16)

    xs = x[0]
    for l in range(DEPTH):
        qp, kp, vp, fz, rq, rk, rv, rz, mq, mz = _proj_call(
            xs, pre_norm[l][None, :], w_in_p[l], b_f_p[l][None, :], tri, tm)
        fox = _fox_call(qp, kp, vp, bq, bk, sub)
        ret = _ret_call(rq, rk, rv, tabs, tb)
        xs = _out_call(xs, fox, fz, ret, rz, mq, mz, mk, mv, l, w_out_b[l],
                       post_norm[l][None, :], tm)
    return xs[None]
```

```python
import functools

import numpy as np
import jax
import jax.numpy as jnp
from jax import lax
from jax.experimental import pallas as pl
from jax.experimental.pallas import tpu as pltpu

F32 = jnp.float32
BF16 = jnp.bfloat16

D_MODEL = 1024
DEPTH = 4
N_MEM = 256
HEAD_DIM = 64
FOX_HEADS = 8
RET_HEADS = 4
MEM_HEADS = 4
FOX_W = FOX_HEADS * HEAD_DIM
RET_W = RET_HEADS * HEAD_DIM
MEM_W = MEM_HEADS * HEAD_DIM
D_MIX = FOX_W + RET_W + MEM_W
RET_CHUNK = 128
ROPE_BASE = 10000.0
EPS = 1e-6
QK_SCALE = HEAD_DIM ** -0.5

LANES = 128
LOG2E = 1.4426950408889634
NEG_BIG = -1e30

OFF_FQ = 0
OFF_FK = OFF_FQ + FOX_W
OFF_FV = OFF_FK + FOX_W
OFF_FZ = OFF_FV + FOX_W
OFF_RQ = OFF_FZ + FOX_W
OFF_RK = OFF_RQ + RET_W
OFF_RV = OFF_RK + RET_W
OFF_RZ = OFF_RV + RET_W
OFF_MQ = OFF_RZ + RET_W
OFF_MZ = OFF_MQ + MEM_W
OFF_FF = OFF_MZ + MEM_W
D_IN_PAD = OFF_FF + LANES

X_LANE = HEAD_DIM
VT_ROWS = 80

VMEM_LIMIT = 56 * 1024 * 1024


def _split3(x):
    hi = x.astype(BF16).astype(F32)
    r1 = x - hi
    mid = r1.astype(BF16).astype(F32)
    lo = (r1 - mid).astype(BF16).astype(F32)
    return hi, mid, lo


def _log_sigmoid(x):
    return -(jnp.maximum(-x, 0.0) + jnp.log1p(jnp.exp(-jnp.abs(x))))


def _silu(z):
    return z * (1.0 / (1.0 + jnp.exp(-z)))


def _proj_kernel(x_ref, g_ref, w_ref, bf_ref, tri_ref,
                 qp_ref, kp_ref, vp_ref, fz_ref, rq_ref, rk_ref, rv_ref, rz_ref,
                 mq_ref, mz_ref, carry_ref):
    i = pl.program_id(0)

    @pl.when(i == 0)
    def _():
        carry_ref[...] = jnp.zeros_like(carry_ref)

    x = x_ref[...]
    ms = jnp.mean(x * x, axis=-1, keepdims=True)
    h = (x * lax.rsqrt(ms + EPS) * g_ref[...]).astype(BF16)

    def proj(off, width):
        return jnp.dot(h, w_ref[0, :, off:off + width], preferred_element_type=F32)

    tm = x.shape[0]
    lane = lax.broadcasted_iota(jnp.int32, (tm, LANES), 1)

    lf = _log_sigmoid(proj(OFF_FF, LANES) + bf_ref[...])
    tri = tri_ref[...]
    c = carry_ref[...]
    for part in _split3(lf):
        c = c + jnp.dot(tri, part.astype(BF16), preferred_element_type=F32)
    carry_ref[...] = c[tm - 1:tm, :]
    c_parts = _split3(c * LOG2E)

    fq = proj(OFF_FQ, FOX_W) * (QK_SCALE * LOG2E)
    fk = proj(OFF_FK, FOX_W)
    fv = proj(OFF_FV, FOX_W)
    for hd in range(FOX_HEADS):
        lo = LANES * (hd // 2)

        def head_chunk(a):
            ch = a[:, lo:lo + LANES]
            if hd % 2:
                ch = pltpu.roll(ch, HEAD_DIM, 1)
            return jnp.where(lane < HEAD_DIM, ch, 0.0)

        b0, b1, b2 = (p[:, hd:hd + 1] for p in c_parts)
        q = head_chunk(fq)
        q = jnp.where(lane == X_LANE, b0, q)
        q = jnp.where(lane == X_LANE + 1, b1, q)
        q = jnp.where(lane == X_LANE + 2, b2, q)
        q = jnp.where((lane >= X_LANE + 3) & (lane < X_LANE + 6), 1.0, q)
        k = head_chunk(fk)
        k = jnp.where((lane >= X_LANE) & (lane < X_LANE + 3), 1.0, k)
        k = jnp.where(lane == X_LANE + 3, -b0, k)
        k = jnp.where(lane == X_LANE + 4, -b1, k)
        k = jnp.where(lane == X_LANE + 5, -b2, k)
        v = jnp.where(lane == X_LANE, 1.0, head_chunk(fv))
        qp_ref[hd] = q.T.astype(BF16)
        kp_ref[hd] = k.astype(BF16)
        vp_ref[hd] = v.T[:VT_ROWS, :].astype(BF16)

    fz_ref[...] = proj(OFF_FZ, FOX_W)
    rq_ref[...] = proj(OFF_RQ, RET_W)
    rk_ref[...] = proj(OFF_RK, RET_W)
    rv_ref[...] = proj(OFF_RV, RET_W).astype(BF16)
    rz_ref[...] = proj(OFF_RZ, RET_W)
    mq_ref[...] = (proj(OFF_MQ, MEM_W) * QK_SCALE).astype(BF16)
    mz_ref[...] = proj(OFF_MZ, MEM_W)


def _proj_call(x, g_pre, w_in_p, layer, b_f_p, tri, tm):
    S = x.shape[0]
    row = lambda w: pl.BlockSpec((tm, w), lambda i: (i, 0))
    const = lambda shape: pl.BlockSpec(shape, lambda i: (0,) * len(shape))
    head = pl.BlockSpec((FOX_HEADS, tm, LANES), lambda i: (0, i, 0))
    head_t = pl.BlockSpec((FOX_HEADS, LANES, tm), lambda i: (0, 0, i))
    head_vt = pl.BlockSpec((FOX_HEADS, VT_ROWS, tm), lambda i: (0, 0, i))
    out_shape = (
        jax.ShapeDtypeStruct((FOX_HEADS, LANES, S), BF16),
        jax.ShapeDtypeStruct((FOX_HEADS, S, LANES), BF16),
        jax.ShapeDtypeStruct((FOX_HEADS, VT_ROWS, S), BF16),
        jax.ShapeDtypeStruct((S, FOX_W), F32),
        jax.ShapeDtypeStruct((S, RET_W), F32),
        jax.ShapeDtypeStruct((S, RET_W), F32),
        jax.ShapeDtypeStruct((S, RET_W), BF16),
        jax.ShapeDtypeStruct((S, RET_W), F32),
        jax.ShapeDtypeStruct((S, MEM_W), BF16),
        jax.ShapeDtypeStruct((S, MEM_W), F32),
    )
    return pl.pallas_call(
        _proj_kernel,
        grid=(S // tm,),
        in_specs=[row(D_MODEL), const((1, D_MODEL)),
                  pl.BlockSpec((1, D_MODEL, D_IN_PAD), lambda i: (layer, 0, 0)),
                  const((1, LANES)), const((tm, tm))],
        out_specs=(head_t, head, head_vt, row(FOX_W), row(RET_W), row(RET_W), row(RET_W),
                   row(RET_W), row(MEM_W), row(MEM_W)),
        out_shape=out_shape,
        scratch_shapes=[pltpu.VMEM((1, LANES), F32)],
        compiler_params=pltpu.CompilerParams(
            dimension_semantics=("arbitrary",), vmem_limit_bytes=VMEM_LIMIT),
        name="proj",
    )(x, g_pre, w_in_p, b_f_p, tri)


def _max_rows(s):
    rows = s.shape[0]
    while rows % 16 == 0:
        rows //= 2
        s = jnp.maximum(s[:rows, :], s[rows:, :])
    return jnp.max(s, axis=0, keepdims=True)


def _fox_kernel(qt_ref, k_ref, vt_ref, o_ref, m_ref, acc_ref, sa_ref, sb_ref, *, bq, bk, sub):
    assert bk == bq
    i = pl.program_id(1)
    nsub = bq // sub
    chains = [(hh, sb) for hh in range(2) for sb in range(nsub)]

    m_ref[...] = jnp.full(m_ref.shape, NEG_BIG, F32)
    acc_ref[...] = jnp.zeros(acc_ref.shape, F32)

    def scores(c, kb):
        hh, sb = c
        k = k_ref[hh, pl.ds(pl.multiple_of(kb * bk, bk), bk), :]
        return jnp.dot(k, qt_ref[hh, :, sb * sub:(sb + 1) * sub], preferred_element_type=F32)

    def finish(c, s, kb, masked):
        hh, sb = c
        cols = slice(sb * sub, (sb + 1) * sub)
        nk = bk
        if masked:
            nk = (sb + 1) * sub
            key = lax.broadcasted_iota(jnp.int32, (nk, sub), 0)
            qry = lax.broadcasted_iota(jnp.int32, (nk, sub), 1) + sb * sub
            s = jnp.where(key <= qry, s[:nk, :], NEG_BIG)
        vt = vt_ref[hh, :, pl.ds(pl.multiple_of(kb * bk, bk), nk)]
        m = m_ref[hh, :, cols]
        m_new = jnp.maximum(m, _max_rows(s))
        p = jnp.exp2(s - m_new)
        alpha = jnp.exp2(m - m_new)
        pv = jnp.dot(vt, p.astype(BF16), preferred_element_type=F32)
        acc_ref[hh, :, cols] = alpha * acc_ref[hh, :, cols] + pv
        m_ref[hh, :, cols] = m_new

    def step(kb, cur, nxt, masked=False, last=False):
        for n, c in enumerate(chains):
            if not last:
                nxt[n] = scores(c, kb + 1)
            finish(c, cur[n], kb, masked)

    for n, c in enumerate(chains):
        sa_ref[n] = scores(c, 0)

    def body(j, carry):
        step(2 * j, sa_ref, sb_ref)
        step(2 * j + 1, sb_ref, sa_ref)
        return carry

    lax.fori_loop(0, i // 2, body, 0)

    @pl.when(i % 2 == 0)
    def _():
        step(i, sa_ref, None, masked=True, last=True)

    @pl.when(i % 2 == 1)
    def _():
        step(i - 1, sa_ref, sb_ref)
        step(i, sb_ref, None, masked=True, last=True)

    lane = lax.broadcasted_iota(jnp.int32, (bq, LANES), 1)
    outs = []
    for hh in range(2):
        pad = jnp.zeros((LANES - VT_ROWS, bq), F32)
        o = jnp.concatenate([acc_ref[hh], pad], axis=0).T
        outs.append(o / o[:, X_LANE:X_LANE + 1])
    o_ref[...] = jnp.where(lane < HEAD_DIM, outs[0], pltpu.roll(outs[1], HEAD_DIM, 1))


def _fox_call(qt, kp, vt, bq, bk, sub):
    _, S, _ = kp.shape
    k_spec = pl.BlockSpec((2, S, LANES), lambda p, i: (p, 0, 0), pipeline_mode=pl.Buffered(1))
    vt_spec = pl.BlockSpec((2, VT_ROWS, S), lambda p, i: (p, 0, 0), pipeline_mode=pl.Buffered(1))
    return pl.pallas_call(
        functools.partial(_fox_kernel, bq=bq, bk=bk, sub=sub),
        grid=(FOX_HEADS // 2, S // bq),
        in_specs=[pl.BlockSpec((2, LANES, bq), lambda p, i: (p, 0, i)), k_spec, vt_spec],
        out_specs=pl.BlockSpec((bq, LANES), lambda p, i: (i, p)),
        out_shape=jax.ShapeDtypeStruct((S, FOX_W), F32),
        scratch_shapes=[pltpu.VMEM((2, 1, bq), F32), pltpu.VMEM((2, VT_ROWS, bq), F32),
                        pltpu.VMEM((2 * bq // sub, bk, sub), F32),
                        pltpu.VMEM((2 * bq // sub, bk, sub), F32)],
        compiler_params=pltpu.CompilerParams(
            dimension_semantics=("arbitrary", "arbitrary"), vmem_limit_bytes=VMEM_LIMIT),
        name="fox",
    )(qt, kp, vt)


def _ret_kernel(q_ref, k_ref, v_ref, cos_ref, sin_ref, dintra_ref, xi_ref, zeta_ref, g_ref,
                o_ref, r_ref, *, tb):
    j = pl.program_id(1)

    @pl.when(j == 0)
    def _():
        r_ref[...] = jnp.zeros_like(r_ref)

    C = RET_CHUNK
    half = HEAD_DIM // 2
    lane = lax.broadcasted_iota(jnp.int32, (tb, LANES), 1)
    first_half = (lane % HEAD_DIM) < half
    cos = cos_ref[...]
    sin = sin_ref[...]

    def rotary(x):
        swapped = jnp.where(first_half, pltpu.roll(x, LANES - half, 1), pltpu.roll(x, half, 1))
        return x * cos + swapped * sin

    q = rotary(q_ref[...])
    k = rotary(k_ref[...]) * QK_SCALE
    v = v_ref[...]
    xi = xi_ref[0]
    zeta = zeta_ref[0]
    g = g_ref[0]
    lane_c = lax.broadcasted_iota(jnp.int32, (C, LANES), 1)
    head0 = lane_c < HEAD_DIM
    same_head = (lax.broadcasted_iota(jnp.int32, (LANES, LANES), 0) // HEAD_DIM
                 == lax.broadcasted_iota(jnp.int32, (LANES, LANES), 1) // HEAD_DIM)
    nt = (((1,), (1,)), ((), ()))
    tn = (((0,), (0,)), ((), ()))

    R = r_ref[...]
    for n in range(tb // C):
        qc = q[n * C:(n + 1) * C, :]
        kc = k[n * C:(n + 1) * C, :]
        vc = v[n * C:(n + 1) * C, :]
        kcb = kc.astype(BF16)
        o_intra = []
        for hh in range(2):
            qm = jnp.where(head0 if hh == 0 else ~head0, qc, 0.0).astype(BF16)
            a = lax.dot_general(qm, kcb, nt, preferred_element_type=F32) * dintra_ref[0, hh]
            o_intra.append(jnp.dot(a.astype(BF16), vc, preferred_element_type=F32))
        o = jnp.where(head0, o_intra[0], o_intra[1])
        o = o + jnp.dot((qc * xi).astype(BF16), R.astype(BF16), preferred_element_type=F32)
        kv = lax.dot_general((kc * zeta).astype(BF16), vc, tn, preferred_element_type=F32)
        R = g * R + jnp.where(same_head, kv, 0.0)
        sq = o * o
        s0 = jnp.sum(jnp.where(head0, sq, 0.0), axis=1, keepdims=True)
        s1 = jnp.sum(jnp.where(head0, 0.0, sq), axis=1, keepdims=True)
        ms = jnp.where(head0, s0, s1) * (1.0 / HEAD_DIM)
        o_ref[n * C:(n + 1) * C, :] = o * lax.rsqrt(ms + EPS)
    r_ref[...] = R


def _ret_call(rq, rk, rv, tabs, tb):
    S = rq.shape[0]
    cos, sin, dintra, xi, zeta, g = tabs
    row = pl.BlockSpec((tb, LANES), lambda p, j: (j, p))
    pair = lambda shape: pl.BlockSpec((1,) + shape, lambda p, j: (p,) + (0,) * len(shape))
    return pl.pallas_call(
        functools.partial(_ret_kernel, tb=tb),
        grid=(RET_HEADS // 2, S // tb),
        in_specs=[row, row, row, row, row,
                  pair((2, RET_CHUNK, RET_CHUNK)), pair((RET_CHUNK, LANES)),
                  pair((RET_CHUNK, LANES)), pair((LANES, LANES))],
        out_specs=row,
        out_shape=jax.ShapeDtypeStruct((S, RET_W), F32),
        scratch_shapes=[pltpu.VMEM((LANES, LANES), F32)],
        compiler_params=pltpu.CompilerParams(
            dimension_semantics=("arbitrary", "arbitrary"), vmem_limit_bytes=VMEM_LIMIT),
        name="ret",
    )(rq, rk, rv, cos, sin, dintra, xi, zeta, g)


def _ret_tables(S):
    half = HEAD_DIM // 2
    C = RET_CHUNK
    pos = jnp.arange(S, dtype=F32)
    freqs = ROPE_BASE ** (-jnp.arange(half, dtype=F32) / half)
    ang = pos[:, None] * freqs[None, :]
    cos, sin = jnp.cos(ang), jnp.sin(ang)
    cos_t = jnp.tile(jnp.concatenate([cos, cos], axis=1), (1, RET_HEADS))
    sin_t = jnp.tile(jnp.concatenate([-sin, sin], axis=1), (1, RET_HEADS))
    log_gamma = jnp.log(1.0 - 2.0 ** (-5.0 - jnp.arange(RET_HEADS, dtype=F32)))
    idx = jnp.arange(C, dtype=F32)
    diff = idx[:, None] - idx[None, :]
    decay_intra = jnp.where(diff >= 0, jnp.exp(log_gamma[:, None, None] * jnp.maximum(diff, 0.0)), 0.0)
    zeta = jnp.exp(log_gamma[:, None] * (C - 1 - idx)[None, :])
    xi = jnp.exp(log_gamma[:, None] * (idx + 1)[None, :])
    decay_chunk = jnp.exp(log_gamma * C)
    npair = RET_HEADS // 2
    dintra = decay_intra.reshape(npair, 2, C, C)
    lanes = lambda t: jnp.repeat(t.reshape(npair, 2, C).transpose(0, 2, 1), HEAD_DIM, axis=2)
    gd = jnp.repeat(decay_chunk.reshape(npair, 2), HEAD_DIM, axis=1)
    blk = jnp.arange(LANES) // HEAD_DIM
    g = jnp.where(blk[:, None] == blk[None, :], gd[:, :, None], 0.0)
    return cos_t, sin_t, dintra, lanes(xi), lanes(zeta), g


def _memkv_kernel(mem_ref, g_ref, w_ref, mk_ref, mv_ref):
    x = mem_ref[...]
    ms = jnp.mean(x * x, axis=-1, keepdims=True)
    h = (x * lax.rsqrt(ms + EPS) * g_ref[0]).astype(BF16)
    kv = jnp.dot(h, w_ref[0], preferred_element_type=F32)
    mk_ref[0] = kv[:, :MEM_W].astype(BF16)
    mv_ref[0] = kv[:, MEM_W:].astype(BF16)


def _memkv_call(mem, g_mem, w_mkv):
    M = mem.shape[0]
    return pl.pallas_call(
        _memkv_kernel,
        grid=(DEPTH,),
        in_specs=[pl.BlockSpec((M, D_MODEL), lambda l: (0, 0)),
                  pl.BlockSpec((1, 1, D_MODEL), lambda l: (l, 0, 0)),
                  pl.BlockSpec((1, D_MODEL, 2 * MEM_W), lambda l: (l, 0, 0))],
        out_specs=(pl.BlockSpec((1, M, MEM_W), lambda l: (l, 0, 0)),
                   pl.BlockSpec((1, M, MEM_W), lambda l: (l, 0, 0))),
        out_shape=(jax.ShapeDtypeStruct((DEPTH, M, MEM_W), BF16),
                   jax.ShapeDtypeStruct((DEPTH, M, MEM_W), BF16)),
        compiler_params=pltpu.CompilerParams(
            dimension_semantics=("arbitrary",), vmem_limit_bytes=VMEM_LIMIT),
        name="memkv",
    )(mem, g_mem, w_mkv)


def _out_kernel(x_ref, fox_ref, fz_ref, ret_ref, rz_ref, mq_ref, mz_ref, mk_ref, mv_ref,
                w_ref, g_ref, o_ref):
    tm = x_ref.shape[0]
    lane = lax.broadcasted_iota(jnp.int32, (tm, LANES), 1)
    head0 = lane < HEAD_DIM
    nt = (((1,), (1,)), ((), ()))

    mxa = []
    for pr in range(MEM_HEADS // 2):
        cols = slice(pr * LANES, (pr + 1) * LANES)
        q2 = mq_ref[:, cols]
        mk2 = mk_ref[0, :, cols]
        mv2 = mv_ref[0, :, cols]
        o_h = []
        for hh in range(2):
            qm = jnp.where(head0 if hh == 0 else ~head0, q2, jnp.zeros_like(q2))
            s = lax.dot_general(qm, mk2, nt, preferred_element_type=F32)
            e = jnp.exp(s - jnp.max(s, axis=1, keepdims=True))
            p = e / jnp.sum(e, axis=1, keepdims=True)
            o_h.append(jnp.dot(p.astype(BF16), mv2, preferred_element_type=F32))
        mxa.append(jnp.where(head0, o_h[0], o_h[1]))
    mxa = jnp.concatenate(mxa, axis=1)

    y = jnp.concatenate([
        (fox_ref[...] * _silu(fz_ref[...])).astype(BF16),
        (ret_ref[...] * _silu(rz_ref[...])).astype(BF16),
        (mxa * _silu(mz_ref[...])).astype(BF16),
    ], axis=1)
    o = jnp.dot(y, w_ref[0], preferred_element_type=F32)
    ms = jnp.mean(o * o, axis=-1, keepdims=True)
    o_ref[...] = x_ref[...] + o * lax.rsqrt(ms + EPS) * g_ref[...]


def _out_call(x, fox, fz, ret, rz, mq, mz, mk, mv, layer, w_out, g_post, tm):
    S = x.shape[0]
    M = mk.shape[1]
    row = lambda w: pl.BlockSpec((tm, w), lambda i: (i, 0))
    const = lambda shape: pl.BlockSpec(shape, lambda i: (0,) * len(shape))
    mem_spec = pl.BlockSpec((1, M, MEM_W), lambda i: (layer, 0, 0))
    return pl.pallas_call(
        _out_kernel,
        grid=(S // tm,),
        in_specs=[row(D_MODEL), row(FOX_W), row(FOX_W), row(RET_W), row(RET_W), row(MEM_W),
                  row(MEM_W), mem_spec, mem_spec,
                  pl.BlockSpec((1, D_MIX, D_MODEL), lambda i: (layer, 0, 0)), const((1, D_MODEL))],
        out_specs=row(D_MODEL),
        out_shape=jax.ShapeDtypeStruct((S, D_MODEL), F32),
        compiler_params=pltpu.CompilerParams(
            dimension_semantics=("arbitrary",), vmem_limit_bytes=VMEM_LIMIT),
        name="out",
    )(x, fox, fz, ret, rz, mq, mz, mk, mv, w_out, g_post)


def _tiles(S):
    tm = min(512, S)
    bq = min(1024, S)
    bk = bq
    sub = min(256, bq)
    tb = min(1024, S)
    return tm, bq, bk, sub, tb


def _reorder_w_in(w_in):
    ff0 = 4 * FOX_W
    main = jnp.concatenate([w_in[:, :, :ff0], w_in[:, :, ff0 + FOX_HEADS:]], axis=2)
    ff = w_in[:, :, ff0:ff0 + FOX_HEADS]
    pad = jnp.zeros(ff.shape[:2] + (LANES - FOX_HEADS,), w_in.dtype)
    return jnp.concatenate([main, ff, pad], axis=2).astype(BF16)


def kernel(x, mem, w_in, b_f, w_out, w_mem_kv, pre_norm, post_norm, mem_norm):
    B, S, _ = x.shape
    assert B == 1
    tm, bq, bk, sub, tb = _tiles(S)
    w_in_p = _reorder_w_in(w_in)
    b_f_p = jnp.pad(b_f, ((0, 0), (0, LANES - FOX_HEADS)))
    tri = jnp.tril(jnp.ones((tm, tm), F32)).astype(BF16)
    tabs = _ret_tables(S)
    mk, mv = _memkv_call(mem[0], mem_norm[:, None, :], w_mem_kv.astype(BF16))
    w_out_b = w_out.astype(BF16)

    xs = x[0]
    for l in range(DEPTH):
        qp, kp, vp, fz, rq, rk, rv, rz, mq, mz = _proj_call(
            xs, pre_norm[l][None, :], w_in_p, l, b_f_p[l][None, :], tri, tm)
        fox = _fox_call(qp, kp, vp, bq, bk, sub)
        ret = _ret_call(rq, rk, rv, tabs, tb)
        xs = _out_call(xs, fox, fz, ret, rz, mq, mz, mk, mv, l, w_out_b,
                       post_norm[l][None, :], tm)
    return xs[None]
```

```python
import functools

import numpy as np
import jax
import jax.numpy as jnp
from jax import lax
from jax.experimental import pallas as pl
from jax.experimental.pallas import tpu as pltpu

F32 = jnp.float32
BF16 = jnp.bfloat16

D_MODEL = 1024
DEPTH = 4
N_MEM = 256
HEAD_DIM = 64
FOX_HEADS = 8
RET_HEADS = 4
MEM_HEADS = 4
FOX_W = FOX_HEADS * HEAD_DIM
RET_W = RET_HEADS * HEAD_DIM
MEM_W = MEM_HEADS * HEAD_DIM
D_MIX = FOX_W + RET_W + MEM_W
RET_CHUNK = 128
ROPE_BASE = 10000.0
EPS = 1e-6
QK_SCALE = HEAD_DIM ** -0.5

LANES = 128
LOG2E = 1.4426950408889634
NEG_BIG = -1e30

OFF_FQ = 0
OFF_FK = OFF_FQ + FOX_W
OFF_FV = OFF_FK + FOX_W
OFF_FZ = OFF_FV + FOX_W
OFF_RQ = OFF_FZ + FOX_W
OFF_RK = OFF_RQ + RET_W
OFF_RV = OFF_RK + RET_W
OFF_RZ = OFF_RV + RET_W
OFF_MQ = OFF_RZ + RET_W
OFF_MZ = OFF_MQ + MEM_W
OFF_FF = OFF_MZ + MEM_W

X_LANE = HEAD_DIM
VT_ROWS = 80

VMEM_LIMIT = 56 * 1024 * 1024


def _split3(x):
    hi = x.astype(BF16).astype(F32)
    r1 = x - hi
    mid = r1.astype(BF16).astype(F32)
    lo = (r1 - mid).astype(BF16).astype(F32)
    return hi, mid, lo


def _log_sigmoid(x):
    return -(jnp.maximum(-x, 0.0) + jnp.log1p(jnp.exp(-jnp.abs(x))))


def _silu(z):
    return z * (1.0 / (1.0 + jnp.exp(-z)))


def _proj_kernel(x_ref, g_ref, wh_ref, wt_ref, wf_ref, bf_ref, tri_ref,
                 qp_ref, kp_ref, vp_ref, fz_ref, rq_ref, rk_ref, rv_ref, rz_ref,
                 mq_ref, mz_ref, carry_ref):
    i = pl.program_id(0)

    @pl.when(i == 0)
    def _():
        carry_ref[...] = jnp.zeros_like(carry_ref)

    x = x_ref[...]
    ms = jnp.mean(x * x, axis=-1, keepdims=True)
    h = (x * lax.rsqrt(ms + EPS) * g_ref[...]).astype(BF16)

    def proj(off, width):
        if off < OFF_RQ:
            w = wh_ref[0, :, off:off + width]
        elif off < OFF_FF:
            w = wt_ref[0, :, off - OFF_RQ:off - OFF_RQ + width]
        else:
            w = wf_ref[0]
        return jnp.dot(h, w, preferred_element_type=F32)

    tm = x.shape[0]
    lane = lax.broadcasted_iota(jnp.int32, (tm, LANES), 1)

    lf = _log_sigmoid(proj(OFF_FF, LANES) + bf_ref[...])
    tri = tri_ref[...]
    c = carry_ref[...]
    for part in _split3(lf):
        c = c + jnp.dot(tri, part.astype(BF16), preferred_element_type=F32)
    carry_ref[...] = c[tm - 1:tm, :]
    c_parts = _split3(c * LOG2E)

    fq = proj(OFF_FQ, FOX_W) * (QK_SCALE * LOG2E)
    fk = proj(OFF_FK, FOX_W)
    fv = proj(OFF_FV, FOX_W)
    for hd in range(FOX_HEADS):
        lo = LANES * (hd // 2)

        def head_chunk(a):
            ch = a[:, lo:lo + LANES]
            if hd % 2:
                ch = pltpu.roll(ch, HEAD_DIM, 1)
            return jnp.where(lane < HEAD_DIM, ch, 0.0)

        b0, b1, b2 = (p[:, hd:hd + 1] for p in c_parts)
        q = head_chunk(fq)
        q = jnp.where(lane == X_LANE, b0, q)
        q = jnp.where(lane == X_LANE + 1, b1, q)
        q = jnp.where(lane == X_LANE + 2, b2, q)
        q = jnp.where((lane >= X_LANE + 3) & (lane < X_LANE + 6), 1.0, q)
        k = head_chunk(fk)
        k = jnp.where((lane >= X_LANE) & (lane < X_LANE + 3), 1.0, k)
        k = jnp.where(lane == X_LANE + 3, -b0, k)
        k = jnp.where(lane == X_LANE + 4, -b1, k)
        k = jnp.where(lane == X_LANE + 5, -b2, k)
        v = jnp.where(lane == X_LANE, 1.0, head_chunk(fv))
        qp_ref[hd] = q.T.astype(BF16)
        kp_ref[hd] = k.astype(BF16)
        vp_ref[hd] = v.T[:VT_ROWS, :].astype(BF16)

    fz_ref[...] = proj(OFF_FZ, FOX_W)
    rq_ref[...] = proj(OFF_RQ, RET_W)
    rk_ref[...] = proj(OFF_RK, RET_W)
    rv_ref[...] = proj(OFF_RV, RET_W).astype(BF16)
    rz_ref[...] = proj(OFF_RZ, RET_W)
    mq_ref[...] = (proj(OFF_MQ, MEM_W) * QK_SCALE).astype(BF16)
    mz_ref[...] = proj(OFF_MZ, MEM_W)


def _proj_call(x, g_pre, w_parts, layer, b_f_p, tri, tm):
    S = x.shape[0]
    w_spec = lambda w: pl.BlockSpec((1,) + w.shape[1:], lambda i: (layer, 0, 0))
    row = lambda w: pl.BlockSpec((tm, w), lambda i: (i, 0))
    const = lambda shape: pl.BlockSpec(shape, lambda i: (0,) * len(shape))
    head = pl.BlockSpec((FOX_HEADS, tm, LANES), lambda i: (0, i, 0))
    head_t = pl.BlockSpec((FOX_HEADS, LANES, tm), lambda i: (0, 0, i))
    head_vt = pl.BlockSpec((FOX_HEADS, VT_ROWS, tm), lambda i: (0, 0, i))
    out_shape = (
        jax.ShapeDtypeStruct((FOX_HEADS, LANES, S), BF16),
        jax.ShapeDtypeStruct((FOX_HEADS, S, LANES), BF16),
        jax.ShapeDtypeStruct((FOX_HEADS, VT_ROWS, S), BF16),
        jax.ShapeDtypeStruct((S, FOX_W), F32),
        jax.ShapeDtypeStruct((S, RET_W), F32),
        jax.ShapeDtypeStruct((S, RET_W), F32),
        jax.ShapeDtypeStruct((S, RET_W), BF16),
        jax.ShapeDtypeStruct((S, RET_W), F32),
        jax.ShapeDtypeStruct((S, MEM_W), BF16),
        jax.ShapeDtypeStruct((S, MEM_W), F32),
    )
    return pl.pallas_call(
        _proj_kernel,
        grid=(S // tm,),
        in_specs=[row(D_MODEL), const((1, D_MODEL))] + [w_spec(w) for w in w_parts]
                 + [const((1, LANES)), const((tm, tm))],
        out_specs=(head_t, head, head_vt, row(FOX_W), row(RET_W), row(RET_W), row(RET_W),
                   row(RET_W), row(MEM_W), row(MEM_W)),
        out_shape=out_shape,
        scratch_shapes=[pltpu.VMEM((1, LANES), F32)],
        compiler_params=pltpu.CompilerParams(
            dimension_semantics=("arbitrary",), vmem_limit_bytes=VMEM_LIMIT),
        name="proj",
    )(x, g_pre, *w_parts, b_f_p, tri)


def _max_rows(s):
    rows = s.shape[0]
    while rows % 16 == 0:
        rows //= 2
        s = jnp.maximum(s[:rows, :], s[rows:, :])
    return jnp.max(s, axis=0, keepdims=True)


def _fox_kernel(qt_ref, k_ref, vt_ref, o_ref, m_ref, acc_ref, sa_ref, sb_ref, *, bq, bk, sub):
    assert bk == bq
    i = pl.program_id(1)
    nsub = bq // sub
    chains = [(hh, sb) for hh in range(2) for sb in range(nsub)]

    m_ref[...] = jnp.full(m_ref.shape, NEG_BIG, F32)
    acc_ref[...] = jnp.zeros(acc_ref.shape, F32)

    def scores(c, kb):
        hh, sb = c
        k = k_ref[hh, pl.ds(pl.multiple_of(kb * bk, bk), bk), :]
        return jnp.dot(k, qt_ref[hh, :, sb * sub:(sb + 1) * sub], preferred_element_type=F32)

    def finish(c, s, kb, masked):
        hh, sb = c
        cols = slice(sb * sub, (sb + 1) * sub)
        nk = bk
        if masked:
            nk = (sb + 1) * sub
            key = lax.broadcasted_iota(jnp.int32, (nk, sub), 0)
            qry = lax.broadcasted_iota(jnp.int32, (nk, sub), 1) + sb * sub
            s = jnp.where(key <= qry, s[:nk, :], NEG_BIG)
        vt = vt_ref[hh, :, pl.ds(pl.multiple_of(kb * bk, bk), nk)]
        m = m_ref[hh, :, cols]
        m_new = jnp.maximum(m, _max_rows(s))
        p = jnp.exp2(s - m_new)
        alpha = jnp.exp2(m - m_new)
        pv = jnp.dot(vt, p.astype(BF16), preferred_element_type=F32)
        acc_ref[hh, :, cols] = alpha * acc_ref[hh, :, cols] + pv
        m_ref[hh, :, cols] = m_new

    def step(kb, cur, nxt, masked=False, last=False):
        for n, c in enumerate(chains):
            if not last:
                nxt[n] = scores(c, kb + 1)
            finish(c, cur[n], kb, masked)

    for n, c in enumerate(chains):
        sa_ref[n] = scores(c, 0)

    def body(j, carry):
        step(2 * j, sa_ref, sb_ref)
        step(2 * j + 1, sb_ref, sa_ref)
        return carry

    lax.fori_loop(0, i // 2, body, 0)

    @pl.when(i % 2 == 0)
    def _():
        step(i, sa_ref, None, masked=True, last=True)

    @pl.when(i % 2 == 1)
    def _():
        step(i - 1, sa_ref, sb_ref)
        step(i, sb_ref, None, masked=True, last=True)

    outs = [acc_ref[hh, :HEAD_DIM, :] / acc_ref[hh, X_LANE:X_LANE + 1, :] for hh in range(2)]
    o_ref[...] = jnp.concatenate(outs, axis=0).T


def _fox_call(qt, kp, vt, bq, bk, sub):
    _, S, _ = kp.shape
    k_spec = pl.BlockSpec((2, S, LANES), lambda p, i: (p, 0, 0), pipeline_mode=pl.Buffered(1))
    vt_spec = pl.BlockSpec((2, VT_ROWS, S), lambda p, i: (p, 0, 0), pipeline_mode=pl.Buffered(1))
    return pl.pallas_call(
        functools.partial(_fox_kernel, bq=bq, bk=bk, sub=sub),
        grid=(FOX_HEADS // 2, S // bq),
        in_specs=[pl.BlockSpec((2, LANES, bq), lambda p, i: (p, 0, i)), k_spec, vt_spec],
        out_specs=pl.BlockSpec((bq, LANES), lambda p, i: (i, p)),
        out_shape=jax.ShapeDtypeStruct((S, FOX_W), F32),
        scratch_shapes=[pltpu.VMEM((2, 1, bq), F32), pltpu.VMEM((2, VT_ROWS, bq), F32),
                        pltpu.VMEM((2 * bq // sub, bk, sub), F32),
                        pltpu.VMEM((2 * bq // sub, bk, sub), F32)],
        compiler_params=pltpu.CompilerParams(
            dimension_semantics=("arbitrary", "arbitrary"), vmem_limit_bytes=VMEM_LIMIT),
        name="fox",
    )(qt, kp, vt)


def _ret_kernel(q_ref, k_ref, v_ref, cos_ref, sin_ref, dintra_ref, xi_ref, zeta_ref, g_ref,
                o_ref, r_ref, *, tb):
    j = pl.program_id(1)

    @pl.when(j == 0)
    def _():
        r_ref[...] = jnp.zeros_like(r_ref)

    C = RET_CHUNK
    half = HEAD_DIM // 2
    lane = lax.broadcasted_iota(jnp.int32, (tb, LANES), 1)
    first_half = (lane % HEAD_DIM) < half
    cos = cos_ref[...]
    sin = sin_ref[...]

    def rotary(x):
        swapped = jnp.where(first_half, pltpu.roll(x, LANES - half, 1), pltpu.roll(x, half, 1))
        return x * cos + swapped * sin

    q = rotary(q_ref[...])
    k = rotary(k_ref[...]) * QK_SCALE
    v = v_ref[...]
    xi = xi_ref[0]
    zeta = zeta_ref[0]
    g = g_ref[0]
    lane_c = lax.broadcasted_iota(jnp.int32, (C, LANES), 1)
    head0 = lane_c < HEAD_DIM
    same_head = (lax.broadcasted_iota(jnp.int32, (LANES, LANES), 0) // HEAD_DIM
                 == lax.broadcasted_iota(jnp.int32, (LANES, LANES), 1) // HEAD_DIM)
    nt = (((1,), (1,)), ((), ()))
    tn = (((0,), (0,)), ((), ()))

    chunks = range(tb // C)
    qc = [q[n * C:(n + 1) * C, :] for n in chunks]
    kc = [k[n * C:(n + 1) * C, :] for n in chunks]
    vc = [v[n * C:(n + 1) * C, :] for n in chunks]
    a = [[lax.dot_general(jnp.where(head0 if hh == 0 else ~head0, qc[n], 0.0).astype(BF16),
                          kc[n].astype(BF16), nt, preferred_element_type=F32)
          for hh in range(2)] for n in chunks]
    kv = [lax.dot_general((kc[n] * zeta).astype(BF16), vc[n], tn, preferred_element_type=F32)
          for n in chunks]
    o_intra = [[jnp.dot((a[n][hh] * dintra_ref[0, hh]).astype(BF16), vc[n],
                        preferred_element_type=F32)
                for hh in range(2)] for n in chunks]
    R = r_ref[...]
    for n in chunks:
        o = jnp.where(head0, o_intra[n][0], o_intra[n][1])
        o = o + jnp.dot((qc[n] * xi).astype(BF16), R.astype(BF16), preferred_element_type=F32)
        R = g * R + jnp.where(same_head, kv[n], 0.0)
        sq = o * o
        s0 = jnp.sum(jnp.where(head0, sq, 0.0), axis=1, keepdims=True)
        s1 = jnp.sum(jnp.where(head0, 0.0, sq), axis=1, keepdims=True)
        ms = jnp.where(head0, s0, s1) * (1.0 / HEAD_DIM)
        o_ref[n * C:(n + 1) * C, :] = o * lax.rsqrt(ms + EPS)
    r_ref[...] = R


def _ret_call(rq, rk, rv, tabs, tb):
    S = rq.shape[0]
    cos, sin, dintra, xi, zeta, g = tabs
    row = pl.BlockSpec((tb, LANES), lambda p, j: (j, p))
    pair = lambda shape: pl.BlockSpec((1,) + shape, lambda p, j: (p,) + (0,) * len(shape))
    return pl.pallas_call(
        functools.partial(_ret_kernel, tb=tb),
        grid=(RET_HEADS // 2, S // tb),
        in_specs=[row, row, row, row, row,
                  pair((2, RET_CHUNK, RET_CHUNK)), pair((RET_CHUNK, LANES)),
                  pair((RET_CHUNK, LANES)), pair((LANES, LANES))],
        out_specs=row,
        out_shape=jax.ShapeDtypeStruct((S, RET_W), F32),
        scratch_shapes=[pltpu.VMEM((LANES, LANES), F32)],
        compiler_params=pltpu.CompilerParams(
            dimension_semantics=("arbitrary", "arbitrary"), vmem_limit_bytes=VMEM_LIMIT),
        name="ret",
    )(rq, rk, rv, cos, sin, dintra, xi, zeta, g)


def _ret_tables(S):
    half = HEAD_DIM // 2
    C = RET_CHUNK
    pos = jnp.arange(S, dtype=F32)
    freqs = ROPE_BASE ** (-jnp.arange(half, dtype=F32) / half)
    ang = pos[:, None] * freqs[None, :]
    cos, sin = jnp.cos(ang), jnp.sin(ang)
    cos_t = jnp.tile(jnp.concatenate([cos, cos], axis=1), (1, RET_HEADS))
    sin_t = jnp.tile(jnp.concatenate([-sin, sin], axis=1), (1, RET_HEADS))
    log_gamma = jnp.log(1.0 - 2.0 ** (-5.0 - jnp.arange(RET_HEADS, dtype=F32)))
    idx = jnp.arange(C, dtype=F32)
    diff = idx[:, None] - idx[None, :]
    decay_intra = jnp.where(diff >= 0, jnp.exp(log_gamma[:, None, None] * jnp.maximum(diff, 0.0)), 0.0)
    zeta = jnp.exp(log_gamma[:, None] * (C - 1 - idx)[None, :])
    xi = jnp.exp(log_gamma[:, None] * (idx + 1)[None, :])
    decay_chunk = jnp.exp(log_gamma * C)
    npair = RET_HEADS // 2
    dintra = decay_intra.reshape(npair, 2, C, C)
    lanes = lambda t: jnp.repeat(t.reshape(npair, 2, C).transpose(0, 2, 1), HEAD_DIM, axis=2)
    gd = jnp.repeat(decay_chunk.reshape(npair, 2), HEAD_DIM, axis=1)
    blk = jnp.arange(LANES) // HEAD_DIM
    g = jnp.where(blk[:, None] == blk[None, :], gd[:, :, None], 0.0)
    return cos_t, sin_t, dintra, lanes(xi), lanes(zeta), g


def _memkv_kernel(mem_ref, g_ref, w_ref, mk_ref, mv_ref):
    x = mem_ref[...]
    ms = jnp.mean(x * x, axis=-1, keepdims=True)
    h = (x * lax.rsqrt(ms + EPS) * g_ref[0]).astype(BF16)
    kv = jnp.dot(h, w_ref[0], preferred_element_type=F32)
    mk_ref[0] = kv[:, :MEM_W].astype(BF16)
    mv_ref[0] = kv[:, MEM_W:].astype(BF16)


def _memkv_call(mem, g_mem, w_mkv):
    M = mem.shape[0]
    return pl.pallas_call(
        _memkv_kernel,
        grid=(DEPTH,),
        in_specs=[pl.BlockSpec((M, D_MODEL), lambda l: (0, 0)),
                  pl.BlockSpec((1, 1, D_MODEL), lambda l: (l, 0, 0)),
                  pl.BlockSpec((1, D_MODEL, 2 * MEM_W), lambda l: (l, 0, 0))],
        out_specs=(pl.BlockSpec((1, M, MEM_W), lambda l: (l, 0, 0)),
                   pl.BlockSpec((1, M, MEM_W), lambda l: (l, 0, 0))),
        out_shape=(jax.ShapeDtypeStruct((DEPTH, M, MEM_W), BF16),
                   jax.ShapeDtypeStruct((DEPTH, M, MEM_W), BF16)),
        compiler_params=pltpu.CompilerParams(
            dimension_semantics=("arbitrary",), vmem_limit_bytes=VMEM_LIMIT),
        name="memkv",
    )(mem, g_mem, w_mkv)


def _out_kernel(x_ref, fox_ref, fz_ref, ret_ref, rz_ref, mq_ref, mz_ref, mk_ref, mv_ref,
                w_ref, g_ref, o_ref):
    tm = x_ref.shape[0]
    lane = lax.broadcasted_iota(jnp.int32, (tm, LANES), 1)
    head0 = lane < HEAD_DIM
    nt = (((1,), (1,)), ((), ()))

    mxa = []
    for pr in range(MEM_HEADS // 2):
        cols = slice(pr * LANES, (pr + 1) * LANES)
        q2 = mq_ref[:, cols]
        mk2 = mk_ref[0, :, cols]
        mv2 = mv_ref[0, :, cols]
        o_h = []
        for hh in range(2):
            qm = jnp.where(head0 if hh == 0 else ~head0, q2, jnp.zeros_like(q2))
            s = lax.dot_general(qm, mk2, nt, preferred_element_type=F32)
            e = jnp.exp(s - jnp.max(s, axis=1, keepdims=True))
            p = e / jnp.sum(e, axis=1, keepdims=True)
            o_h.append(jnp.dot(p.astype(BF16), mv2, preferred_element_type=F32))
        mxa.append(jnp.where(head0, o_h[0], o_h[1]))
    mxa = jnp.concatenate(mxa, axis=1)

    y = jnp.concatenate([
        (fox_ref[...] * _silu(fz_ref[...])).astype(BF16),
        (ret_ref[...] * _silu(rz_ref[...])).astype(BF16),
        (mxa * _silu(mz_ref[...])).astype(BF16),
    ], axis=1)
    o = jnp.dot(y, w_ref[0], preferred_element_type=F32)
    ms = jnp.mean(o * o, axis=-1, keepdims=True)
    o_ref[...] = x_ref[...] + o * lax.rsqrt(ms + EPS) * g_ref[...]


def _out_call(x, fox, fz, ret, rz, mq, mz, mk, mv, layer, w_out, g_post, tm):
    S = x.shape[0]
    M = mk.shape[1]
    row = lambda w: pl.BlockSpec((tm, w), lambda i: (i, 0))
    const = lambda shape: pl.BlockSpec(shape, lambda i: (0,) * len(shape))
    mem_spec = pl.BlockSpec((1, M, MEM_W), lambda i: (layer, 0, 0))
    return pl.pallas_call(
        _out_kernel,
        grid=(S // tm,),
        in_specs=[row(D_MODEL), row(FOX_W), row(FOX_W), row(RET_W), row(RET_W), row(MEM_W),
                  row(MEM_W), mem_spec, mem_spec,
                  pl.BlockSpec((1, D_MIX, D_MODEL), lambda i: (layer, 0, 0)), const((1, D_MODEL))],
        out_specs=row(D_MODEL),
        out_shape=jax.ShapeDtypeStruct((S, D_MODEL), F32),
        compiler_params=pltpu.CompilerParams(
            dimension_semantics=("arbitrary",), vmem_limit_bytes=VMEM_LIMIT),
        name="out",
    )(x, fox, fz, ret, rz, mq, mz, mk, mv, w_out, g_post)


def _tiles(S):
    tm = min(512, S)
    bq = min(1024, S)
    bk = bq
    sub = min(256, bq)
    tb = min(1024, S)
    return tm, bq, bk, sub, tb


def _split_w_in(w_in):
    ff0 = 4 * FOX_W
    ff = jnp.pad(w_in[:, :, ff0:ff0 + FOX_HEADS], ((0, 0), (0, 0), (0, LANES - FOX_HEADS)))
    return (w_in[:, :, :ff0].astype(BF16), w_in[:, :, ff0 + FOX_HEADS:].astype(BF16),
            ff.astype(BF16))


def kernel(x, mem, w_in, b_f, w_out, w_mem_kv, pre_norm, post_norm, mem_norm):
    B, S, _ = x.shape
    assert B == 1
    tm, bq, bk, sub, tb = _tiles(S)
    w_in_p = _split_w_in(w_in)
    b_f_p = jnp.pad(b_f, ((0, 0), (0, LANES - FOX_HEADS)))
    tri = jnp.tril(jnp.ones((tm, tm), F32)).astype(BF16)
    tabs = _ret_tables(S)
    mk, mv = _memkv_call(mem[0], mem_norm[:, None, :], w_mem_kv.astype(BF16))
    w_out_b = w_out.astype(BF16)

    xs = x[0]
    for l in range(DEPTH):
        qp, kp, vp, fz, rq, rk, rv, rz, mq, mz = _proj_call(
            xs, pre_norm[l][None, :], w_in_p, l, b_f_p[l][None, :], tri, tm)
        fox = _fox_call(qp, kp, vp, bq, bk, sub)
        ret = _ret_call(rq, rk, rv, tabs, tb)
        xs = _out_call(xs, fox, fz, ret, rz, mq, mz, mk, mv, l, w_out_b,
                       post_norm[l][None, :], tm)
    return xs[None]
```

```python
import functools

import jax
import jax.numpy as jnp
from jax import lax
from jax.experimental import pallas as pl
from jax.experimental.pallas import tpu as pltpu

F32 = jnp.float32
BF16 = jnp.bfloat16

D_MODEL = 1024
DEPTH = 4
N_MEM = 256
HEAD_DIM = 64
FOX_HEADS = 8
RET_HEADS = 4
MEM_HEADS = 4
FOX_W = FOX_HEADS * HEAD_DIM
RET_W = RET_HEADS * HEAD_DIM
MEM_W = MEM_HEADS * HEAD_DIM
D_MIX = FOX_W + RET_W + MEM_W
RET_CHUNK = 128
ROPE_BASE = 10000.0
EPS = 1e-6
QK_SCALE = HEAD_DIM ** -0.5

LANES = 128
LOG2E = 1.4426950408889634
NEG_BIG = -1e30

OFF_FQ = 0
OFF_FK = OFF_FQ + FOX_W
OFF_FV = OFF_FK + FOX_W
OFF_FZ = OFF_FV + FOX_W
OFF_RQ = OFF_FZ + FOX_W
OFF_RK = OFF_RQ + RET_W
OFF_RV = OFF_RK + RET_W
OFF_RZ = OFF_RV + RET_W
OFF_MQ = OFF_RZ + RET_W
OFF_MZ = OFF_MQ + MEM_W
OFF_FF = OFF_MZ + MEM_W

X_LANE = HEAD_DIM
VT_ROWS = 80

VMEM_LIMIT = 56 * 1024 * 1024


def _split3(x):
    hi = x.astype(BF16).astype(F32)
    r1 = x - hi
    mid = r1.astype(BF16).astype(F32)
    lo = (r1 - mid).astype(BF16).astype(F32)
    return hi, mid, lo


def _log_sigmoid(x):
    return -(jnp.maximum(-x, 0.0) + jnp.log1p(jnp.exp(-jnp.abs(x))))


def _silu(z):
    return z * (1.0 / (1.0 + jnp.exp(-z)))


def _proj_kernel(x_ref, g_ref, wh_ref, wt_ref, wf_ref, bf_ref, tri_ref,
                 qp_ref, kp_ref, vp_ref, fz_ref, rq_ref, rk_ref, rv_ref, rz_ref,
                 mq_ref, mz_ref, carry_ref):
    i = pl.program_id(0)

    @pl.when(i == 0)
    def _():
        carry_ref[...] = jnp.zeros_like(carry_ref)

    x = x_ref[...]
    ms = jnp.mean(x * x, axis=-1, keepdims=True)
    h = (x * lax.rsqrt(ms + EPS) * g_ref[...]).astype(BF16)

    def proj(off, width):
        if off < OFF_RQ:
            w = wh_ref[0, :, off:off + width]
        elif off < OFF_FF:
            w = wt_ref[0, :, off - OFF_RQ:off - OFF_RQ + width]
        else:
            w = wf_ref[0]
        return jnp.dot(h, w, preferred_element_type=F32)

    tm = x.shape[0]
    lane = lax.broadcasted_iota(jnp.int32, (tm, LANES), 1)

    lf = _log_sigmoid(proj(OFF_FF, LANES) + bf_ref[...])
    fq = proj(OFF_FQ, FOX_W) * (QK_SCALE * LOG2E)
    fk = proj(OFF_FK, FOX_W)
    fv = proj(OFF_FV, FOX_W)
    tri = tri_ref[...]
    c = carry_ref[...]
    for part in _split3(lf):
        c = c + jnp.dot(tri, part.astype(BF16), preferred_element_type=F32)
    carry_ref[...] = c[tm - 1:tm, :]
    c_parts = _split3(c * LOG2E)

    for hd in range(FOX_HEADS):
        lo = LANES * (hd // 2)

        def head_chunk(a):
            ch = a[:, lo:lo + LANES]
            if hd % 2:
                ch = pltpu.roll(ch, HEAD_DIM, 1)
            return jnp.where(lane < HEAD_DIM, ch, 0.0)

        b0, b1, b2 = (p[:, hd:hd + 1] for p in c_parts)
        q = head_chunk(fq)
        q = jnp.where(lane == X_LANE, b0, q)
        q = jnp.where(lane == X_LANE + 1, b1, q)
        q = jnp.where(lane == X_LANE + 2, b2, q)
        q = jnp.where((lane >= X_LANE + 3) & (lane < X_LANE + 6), 1.0, q)
        k = head_chunk(fk)
        k = jnp.where((lane >= X_LANE) & (lane < X_LANE + 3), 1.0, k)
        k = jnp.where(lane == X_LANE + 3, -b0, k)
        k = jnp.where(lane == X_LANE + 4, -b1, k)
        k = jnp.where(lane == X_LANE + 5, -b2, k)
        v = jnp.where(lane == X_LANE, 1.0, head_chunk(fv))
        qp_ref[hd] = q.T.astype(BF16)
        kp_ref[hd] = k.astype(BF16)
        vp_ref[hd] = v.T[:VT_ROWS, :].astype(BF16)

    fz_ref[...] = proj(OFF_FZ, FOX_W)
    rq_ref[...] = proj(OFF_RQ, RET_W)
    rk_ref[...] = proj(OFF_RK, RET_W)
    rv_ref[...] = proj(OFF_RV, RET_W).astype(BF16)
    rz_ref[...] = proj(OFF_RZ, RET_W)
    mq_ref[...] = (proj(OFF_MQ, MEM_W) * QK_SCALE).astype(BF16)
    mz_ref[...] = proj(OFF_MZ, MEM_W)


def _proj_call(x, g_pre, w_parts, layer, b_f_p, tri, tm):
    S = x.shape[0]
    w_spec = lambda w: pl.BlockSpec((1,) + w.shape[1:], lambda i: (layer, 0, 0))
    row = lambda w: pl.BlockSpec((tm, w), lambda i: (i, 0))
    const = lambda shape: pl.BlockSpec(shape, lambda i: (0,) * len(shape))
    head = pl.BlockSpec((FOX_HEADS, tm, LANES), lambda i: (0, i, 0))
    head_t = pl.BlockSpec((FOX_HEADS, LANES, tm), lambda i: (0, 0, i))
    head_vt = pl.BlockSpec((FOX_HEADS, VT_ROWS, tm), lambda i: (0, 0, i))
    out_shape = (
        jax.ShapeDtypeStruct((FOX_HEADS, LANES, S), BF16),
        jax.ShapeDtypeStruct((FOX_HEADS, S, LANES), BF16),
        jax.ShapeDtypeStruct((FOX_HEADS, VT_ROWS, S), BF16),
        jax.ShapeDtypeStruct((S, FOX_W), F32),
        jax.ShapeDtypeStruct((S, RET_W), F32),
        jax.ShapeDtypeStruct((S, RET_W), F32),
        jax.ShapeDtypeStruct((S, RET_W), BF16),
        jax.ShapeDtypeStruct((S, RET_W), F32),
        jax.ShapeDtypeStruct((S, MEM_W), BF16),
        jax.ShapeDtypeStruct((S, MEM_W), F32),
    )
    return pl.pallas_call(
        _proj_kernel,
        grid=(S // tm,),
        in_specs=[row(D_MODEL), const((1, D_MODEL))] + [w_spec(w) for w in w_parts]
                 + [const((1, LANES)), const((tm, tm))],
        out_specs=(head_t, head, head_vt, row(FOX_W), row(RET_W), row(RET_W), row(RET_W),
                   row(RET_W), row(MEM_W), row(MEM_W)),
        out_shape=out_shape,
        scratch_shapes=[pltpu.VMEM((1, LANES), F32)],
        compiler_params=pltpu.CompilerParams(
            dimension_semantics=("arbitrary",), vmem_limit_bytes=VMEM_LIMIT),
        name="proj",
    )(x, g_pre, *w_parts, b_f_p, tri)


def _max_rows(s):
    rows = s.shape[0]
    while rows % 16 == 0:
        rows //= 2
        s = jnp.maximum(s[:rows, :], s[rows:, :])
    return jnp.max(s, axis=0, keepdims=True)


def _fox_kernel(qt_ref, k_ref, vt_ref, o_ref, m_ref, acc_ref, sa_ref, sb_ref, sc_ref, *, bq, bk, sub):
    assert bk == bq
    i = pl.program_id(1)
    nq = pl.num_programs(1)
    nsub = bq // sub
    chains = [(hh, sb) for hh in range(2) for sb in range(nsub)]

    m_ref[...] = jnp.full(m_ref.shape, NEG_BIG, F32)
    acc_ref[...] = jnp.zeros(acc_ref.shape, F32)

    def scores(c, qb, kb, nk=bk):
        hh, sb = c
        k = k_ref[hh, pl.ds(pl.multiple_of(kb * bk, bk), nk), :]
        qt = qt_ref[hh, :, pl.ds(pl.multiple_of(qb * bq + sb * sub, sub), sub)]
        return jnp.dot(k, qt, preferred_element_type=F32)

    def finish(c, s_ref, kb, masked):
        hh, sb = c
        cols = slice(sb * sub, (sb + 1) * sub)
        nk = (sb + 1) * sub if masked else bk
        s = s_ref[:nk, :]
        if masked:
            key = lax.broadcasted_iota(jnp.int32, (nk, sub), 0)
            qry = lax.broadcasted_iota(jnp.int32, (nk, sub), 1) + sb * sub
            s = jnp.where(key <= qry, s, NEG_BIG)
        vt = vt_ref[hh, :, pl.ds(pl.multiple_of(kb * bk, bk), nk)]
        m = m_ref[hh, :, cols]
        m_new = jnp.maximum(m, _max_rows(s))
        p = jnp.exp2(s - m_new)
        alpha = jnp.exp2(m - m_new)
        pv = jnp.dot(vt, p.astype(BF16), preferred_element_type=F32)
        acc_ref[hh, :, cols] = alpha * acc_ref[hh, :, cols] + pv
        m_ref[hh, :, cols] = m_new

    def step(kb, cur, nxt, next_is_diagonal=False):
        for n, c in enumerate(chains):
            nk = (c[1] + 1) * sub if next_is_diagonal else bk
            nxt[n, :nk, :] = scores(c, i, kb + 1, nk)
            finish(c, cur.at[n], kb, False)

    def last_step(cur, nxt):
        nxt_q = (i + 1) % nq
        for n, c in enumerate(chains):
            nxt[n] = scores(c, nxt_q, 0)
            finish(c, cur.at[n], i, True)

    @pl.when(i == 0)
    def _():
        for n, c in enumerate(chains):
            sa_ref[n] = scores(c, 0, 0)

    def block(even_buf, odd_buf, handover, odd_block):
        def body(j, carry):
            step(2 * j, even_buf, odd_buf)
            step(2 * j + 1, odd_buf, even_buf)
            return carry

        lax.fori_loop(0, i // 2, body, 0)
        if odd_block:
            step(i - 1, even_buf, odd_buf, next_is_diagonal=True)
            last_step(odd_buf, handover)
        else:
            last_step(even_buf, handover)

    @pl.when(i % 2 == 0)
    def _():
        block(sa_ref, sb_ref, sc_ref, False)

    @pl.when(i % 2 == 1)
    def _():
        block(sc_ref, sb_ref, sa_ref, True)

    outs = [acc_ref[hh, :HEAD_DIM, :] / acc_ref[hh, X_LANE:X_LANE + 1, :] for hh in range(2)]
    o_ref[...] = jnp.concatenate(outs, axis=0).T


def _fox_call(qt, kp, vt, bq, bk, sub):
    _, S, _ = kp.shape
    once = lambda shape: pl.BlockSpec(shape, lambda p, i: (p, 0, 0), pipeline_mode=pl.Buffered(1))
    score_buf = pltpu.VMEM((2 * bq // sub, bk, sub), F32)
    return pl.pallas_call(
        functools.partial(_fox_kernel, bq=bq, bk=bk, sub=sub),
        grid=(FOX_HEADS // 2, S // bq),
        in_specs=[once((2, LANES, S)), once((2, S, LANES)), once((2, VT_ROWS, S))],
        out_specs=pl.BlockSpec((bq, LANES), lambda p, i: (i, p)),
        out_shape=jax.ShapeDtypeStruct((S, FOX_W), F32),
        scratch_shapes=[pltpu.VMEM((2, 1, bq), F32), pltpu.VMEM((2, VT_ROWS, bq), F32),
                        score_buf, score_buf, score_buf],
        compiler_params=pltpu.CompilerParams(
            dimension_semantics=("arbitrary", "arbitrary"), vmem_limit_bytes=VMEM_LIMIT),
        name="fox",
    )(qt, kp, vt)


def _ret_kernel(q_ref, k_ref, v_ref, cos_ref, sin_ref, dintra_ref, xi_ref, zeta_ref, g_ref,
                o_ref, r_ref, *, tb):
    j = pl.program_id(1)

    @pl.when(j == 0)
    def _():
        r_ref[...] = jnp.zeros_like(r_ref)

    C = RET_CHUNK
    half = HEAD_DIM // 2
    lane = lax.broadcasted_iota(jnp.int32, (tb, LANES), 1)
    first_half = (lane % HEAD_DIM) < half
    cos = cos_ref[...]
    sin = sin_ref[...]

    def rotary(x):
        swapped = jnp.where(first_half, pltpu.roll(x, LANES - half, 1), pltpu.roll(x, half, 1))
        return x * cos + swapped * sin

    q = rotary(q_ref[...])
    k = rotary(k_ref[...]) * QK_SCALE
    v = v_ref[...]
    xi = xi_ref[0]
    zeta = zeta_ref[0]
    g = g_ref[0]
    lane_c = lax.broadcasted_iota(jnp.int32, (C, LANES), 1)
    head0 = lane_c < HEAD_DIM
    same_head = (lax.broadcasted_iota(jnp.int32, (LANES, LANES), 0) // HEAD_DIM
                 == lax.broadcasted_iota(jnp.int32, (LANES, LANES), 1) // HEAD_DIM)
    nt = (((1,), (1,)), ((), ()))
    tn = (((0,), (0,)), ((), ()))

    chunks = range(tb // C)
    qc = [q[n * C:(n + 1) * C, :] for n in chunks]
    kc = [k[n * C:(n + 1) * C, :] for n in chunks]
    vc = [v[n * C:(n + 1) * C, :] for n in chunks]
    a = [[lax.dot_general(jnp.where(head0 if hh == 0 else ~head0, qc[n], 0.0).astype(BF16),
                          kc[n].astype(BF16), nt, preferred_element_type=F32)
          for hh in range(2)] for n in chunks]
    kv = [lax.dot_general((kc[n] * zeta).astype(BF16), vc[n], tn, preferred_element_type=F32)
          for n in chunks]
    o_intra = [[jnp.dot((a[n][hh] * dintra_ref[0, hh]).astype(BF16), vc[n],
                        preferred_element_type=F32)
                for hh in range(2)] for n in chunks]
    R = r_ref[...]
    for n in chunks:
        o = jnp.where(head0, o_intra[n][0], o_intra[n][1])
        o = o + jnp.dot((qc[n] * xi).astype(BF16), R.astype(BF16), preferred_element_type=F32)
        R = g * R + jnp.where(same_head, kv[n], 0.0)
        sq = o * o
        s0 = jnp.sum(jnp.where(head0, sq, 0.0), axis=1, keepdims=True)
        s1 = jnp.sum(jnp.where(head0, 0.0, sq), axis=1, keepdims=True)
        ms = jnp.where(head0, s0, s1) * (1.0 / HEAD_DIM)
        o_ref[n * C:(n + 1) * C, :] = o * lax.rsqrt(ms + EPS)
    r_ref[...] = R


def _ret_call(rq, rk, rv, tabs, tb):
    S = rq.shape[0]
    cos, sin, dintra, xi, zeta, g = tabs
    row = pl.BlockSpec((tb, LANES), lambda p, j: (j, p))
    tab = pl.BlockSpec((tb, LANES), lambda p, j: (j, 0))
    pair = lambda shape: pl.BlockSpec((1,) + shape, lambda p, j: (p,) + (0,) * len(shape))
    return pl.pallas_call(
        functools.partial(_ret_kernel, tb=tb),
        grid=(RET_HEADS // 2, S // tb),
        in_specs=[row, row, row, tab, tab,
                  pair((2, RET_CHUNK, RET_CHUNK)), pair((RET_CHUNK, LANES)),
                  pair((RET_CHUNK, LANES)), pair((LANES, LANES))],
        out_specs=row,
        out_shape=jax.ShapeDtypeStruct((S, RET_W), F32),
        scratch_shapes=[pltpu.VMEM((LANES, LANES), F32)],
        compiler_params=pltpu.CompilerParams(
            dimension_semantics=("arbitrary", "arbitrary"), vmem_limit_bytes=VMEM_LIMIT),
        name="ret",
    )(rq, rk, rv, cos, sin, dintra, xi, zeta, g)


def _ret_tables(S):
    half = HEAD_DIM // 2
    C = RET_CHUNK
    pos = jnp.arange(S, dtype=F32)
    freqs = ROPE_BASE ** (-jnp.arange(half, dtype=F32) / half)
    ang = pos[:, None] * freqs[None, :]
    cos, sin = jnp.cos(ang), jnp.sin(ang)
    cos_t = jnp.tile(jnp.concatenate([cos, cos], axis=1), (1, 2))
    sin_t = jnp.tile(jnp.concatenate([-sin, sin], axis=1), (1, 2))
    log_gamma = jnp.log(1.0 - 2.0 ** (-5.0 - jnp.arange(RET_HEADS, dtype=F32)))
    idx = jnp.arange(C, dtype=F32)
    diff = idx[:, None] - idx[None, :]
    decay_intra = jnp.where(diff >= 0, jnp.exp(log_gamma[:, None, None] * jnp.maximum(diff, 0.0)), 0.0)
    zeta = jnp.exp(log_gamma[:, None] * (C - 1 - idx)[None, :])
    xi = jnp.exp(log_gamma[:, None] * (idx + 1)[None, :])
    decay_chunk = jnp.exp(log_gamma * C)
    npair = RET_HEADS // 2
    dintra = decay_intra.reshape(npair, 2, C, C)
    lanes = lambda t: jnp.repeat(t.reshape(npair, 2, C).transpose(0, 2, 1), HEAD_DIM, axis=2)
    gd = jnp.repeat(decay_chunk.reshape(npair, 2), HEAD_DIM, axis=1)
    blk = jnp.arange(LANES) // HEAD_DIM
    g = jnp.where(blk[:, None] == blk[None, :], gd[:, :, None], 0.0)
    return cos_t, sin_t, dintra, lanes(xi), lanes(zeta), g


def _memkv_kernel(mem_ref, g_ref, w_ref, mk_ref, mv_ref):
    x = mem_ref[...]
    ms = jnp.mean(x * x, axis=-1, keepdims=True)
    h = (x * lax.rsqrt(ms + EPS) * g_ref[0]).astype(BF16)
    kv = jnp.dot(h, w_ref[0], preferred_element_type=F32)
    mk_ref[0] = kv[:, :MEM_W].astype(BF16)
    mv_ref[0] = kv[:, MEM_W:].astype(BF16)


def _memkv_call(mem, g_mem, w_mkv):
    M = mem.shape[0]
    return pl.pallas_call(
        _memkv_kernel,
        grid=(DEPTH,),
        in_specs=[pl.BlockSpec((M, D_MODEL), lambda l: (0, 0)),
                  pl.BlockSpec((1, 1, D_MODEL), lambda l: (l, 0, 0)),
                  pl.BlockSpec((1, D_MODEL, 2 * MEM_W), lambda l: (l, 0, 0))],
        out_specs=(pl.BlockSpec((1, M, MEM_W), lambda l: (l, 0, 0)),
                   pl.BlockSpec((1, M, MEM_W), lambda l: (l, 0, 0))),
        out_shape=(jax.ShapeDtypeStruct((DEPTH, M, MEM_W), BF16),
                   jax.ShapeDtypeStruct((DEPTH, M, MEM_W), BF16)),
        compiler_params=pltpu.CompilerParams(
            dimension_semantics=("arbitrary",), vmem_limit_bytes=VMEM_LIMIT),
        name="memkv",
    )(mem, g_mem, w_mkv)


def _out_kernel(x_ref, fox_ref, fz_ref, ret_ref, rz_ref, mq_ref, mz_ref, mk_ref, mv_ref,
                w_ref, g_ref, o_ref):
    tm = x_ref.shape[0]
    lane = lax.broadcasted_iota(jnp.int32, (tm, LANES), 1)
    head0 = lane < HEAD_DIM
    nt = (((1,), (1,)), ((), ()))

    pairs = range(MEM_HEADS // 2)
    s = []
    for pr in pairs:
        cols = slice(pr * LANES, (pr + 1) * LANES)
        q2 = mq_ref[:, cols]
        for hh in range(2):
            qm = jnp.where(head0 if hh == 0 else ~head0, q2, jnp.zeros_like(q2))
            s.append(lax.dot_general(qm, mk_ref[0, :, cols], nt, preferred_element_type=F32))

    y_fr = jnp.concatenate([
        (fox_ref[...] * _silu(fz_ref[...])).astype(BF16),
        (ret_ref[...] * _silu(rz_ref[...])).astype(BF16),
    ], axis=1)
    o = jnp.dot(y_fr, w_ref[0, :FOX_W + RET_W, :], preferred_element_type=F32)

    mxa = []
    for pr in pairs:
        cols = slice(pr * LANES, (pr + 1) * LANES)
        o_h = []
        for hh in range(2):
            sc = s[2 * pr + hh]
            e = jnp.exp(sc - jnp.max(sc, axis=1, keepdims=True))
            p = e / jnp.sum(e, axis=1, keepdims=True)
            o_h.append(jnp.dot(p.astype(BF16), mv_ref[0, :, cols], preferred_element_type=F32))
        mxa.append(jnp.where(head0, o_h[0], o_h[1]))
    y_m = (jnp.concatenate(mxa, axis=1) * _silu(mz_ref[...])).astype(BF16)
    o = o + jnp.dot(y_m, w_ref[0, FOX_W + RET_W:, :], preferred_element_type=F32)
    ms = jnp.mean(o * o, axis=-1, keepdims=True)
    o_ref[...] = x_ref[...] + o * lax.rsqrt(ms + EPS) * g_ref[...]


def _out_call(x, fox, fz, ret, rz, mq, mz, mk, mv, layer, w_out, g_post, tm):
    S = x.shape[0]
    M = mk.shape[1]
    row = lambda w: pl.BlockSpec((tm, w), lambda i: (i, 0))
    const = lambda shape: pl.BlockSpec(shape, lambda i: (0,) * len(shape))
    mem_spec = pl.BlockSpec((1, M, MEM_W), lambda i: (layer, 0, 0))
    return pl.pallas_call(
        _out_kernel,
        grid=(S // tm,),
        in_specs=[row(D_MODEL), row(FOX_W), row(FOX_W), row(RET_W), row(RET_W), row(MEM_W),
                  row(MEM_W), mem_spec, mem_spec,
                  pl.BlockSpec((1, D_MIX, D_MODEL), lambda i: (layer, 0, 0)), const((1, D_MODEL))],
        out_specs=row(D_MODEL),
        out_shape=jax.ShapeDtypeStruct((S, D_MODEL), F32),
        compiler_params=pltpu.CompilerParams(
            dimension_semantics=("arbitrary",), vmem_limit_bytes=VMEM_LIMIT),
        name="out",
    )(x, fox, fz, ret, rz, mq, mz, mk, mv, w_out, g_post)


def _tiles(S):
    tm = min(512, S)
    bq = min(1024, S)
    bk = bq
    sub = min(256, bq)
    tb = min(1024, S)
    return tm, bq, bk, sub, tb


def _split_w_in(w_in):
    ff0 = 4 * FOX_W
    ff = jnp.pad(w_in[:, :, ff0:ff0 + FOX_HEADS], ((0, 0), (0, 0), (0, LANES - FOX_HEADS)))
    return (w_in[:, :, :ff0].astype(BF16), w_in[:, :, ff0 + FOX_HEADS:].astype(BF16),
            ff.astype(BF16))


def kernel(x, mem, w_in, b_f, w_out, w_mem_kv, pre_norm, post_norm, mem_norm):
    B, S, _ = x.shape
    assert B == 1
    tm, bq, bk, sub, tb = _tiles(S)
    w_in_p = _split_w_in(w_in)
    b_f_p = jnp.pad(b_f, ((0, 0), (0, LANES - FOX_HEADS)))
    tri = jnp.tril(jnp.ones((tm, tm), F32)).astype(BF16)
    tabs = _ret_tables(S)
    mk, mv = _memkv_call(mem[0], mem_norm[:, None, :], w_mem_kv.astype(BF16))
    w_out_b = w_out.astype(BF16)

    xs = x[0]
    for l in range(DEPTH):
        qp, kp, vp, fz, rq, rk, rv, rz, mq, mz = _proj_call(
            xs, pre_norm[l][None, :], w_in_p, l, b_f_p[l][None, :], tri, tm)
        fox = _fox_call(qp, kp, vp, bq, bk, sub)
        ret = _ret_call(rq, rk, rv, tabs, tb)
        xs = _out_call(xs, fox, fz, ret, rz, mq, mz, mk, mv, l, w_out_b,
                       post_norm[l][None, :], tm)
    return xs[None]
```

```python
import functools

import jax
import jax.numpy as jnp
from jax import lax
from jax.experimental import pallas as pl
from jax.experimental.pallas import tpu as pltpu

F32 = jnp.float32
BF16 = jnp.bfloat16

D_MODEL = 1024
DEPTH = 4
N_MEM = 256
HEAD_DIM = 64
FOX_HEADS = 8
RET_HEADS = 4
MEM_HEADS = 4
FOX_W = FOX_HEADS * HEAD_DIM
RET_W = RET_HEADS * HEAD_DIM
MEM_W = MEM_HEADS * HEAD_DIM
D_MIX = FOX_W + RET_W + MEM_W
RET_CHUNK = 128
ROPE_BASE = 10000.0
EPS = 1e-6
QK_SCALE = HEAD_DIM ** -0.5

LANES = 128
LOG2E = 1.4426950408889634
NEG_BIG = -1e30

OFF_FQ = 0
OFF_FK = OFF_FQ + FOX_W
OFF_FV = OFF_FK + FOX_W
OFF_FZ = OFF_FV + FOX_W
OFF_RQ = OFF_FZ + FOX_W
OFF_RK = OFF_RQ + RET_W
OFF_RV = OFF_RK + RET_W
OFF_RZ = OFF_RV + RET_W
OFF_MQ = OFF_RZ + RET_W
OFF_MZ = OFF_MQ + MEM_W
OFF_FF = OFF_MZ + MEM_W

X_LANE = HEAD_DIM
VT_ROWS = 80

VMEM_LIMIT = 56 * 1024 * 1024


def _split3(x):
    hi = x.astype(BF16).astype(F32)
    r1 = x - hi
    mid = r1.astype(BF16).astype(F32)
    lo = (r1 - mid).astype(BF16).astype(F32)
    return hi, mid, lo


def _log_sigmoid(x):
    return -(jnp.maximum(-x, 0.0) + jnp.log1p(jnp.exp(-jnp.abs(x))))


def _silu(z):
    return z * (1.0 / (1.0 + jnp.exp(-z)))


def _proj_kernel(x_ref, g_ref, wh_ref, wt_ref, wf_ref, bf_ref, tri_ref,
                 qp_ref, kp_ref, vp_ref, fz_ref, rq_ref, rk_ref, rv_ref, rz_ref,
                 mq_ref, mz_ref, carry_ref):
    i = pl.program_id(0)

    @pl.when(i == 0)
    def _():
        carry_ref[...] = jnp.zeros_like(carry_ref)

    x = x_ref[...]
    ms = jnp.mean(x * x, axis=-1, keepdims=True)
    h = (x * lax.rsqrt(ms + EPS) * g_ref[...]).astype(BF16)

    def proj(off, width):
        if off < OFF_RQ:
            w = wh_ref[0, :, off:off + width]
        elif off < OFF_FF:
            w = wt_ref[0, :, off - OFF_RQ:off - OFF_RQ + width]
        else:
            w = wf_ref[0]
        return jnp.dot(h, w, preferred_element_type=F32)

    tm = x.shape[0]
    lane = lax.broadcasted_iota(jnp.int32, (tm, LANES), 1)

    lf = _log_sigmoid(proj(OFF_FF, LANES) + bf_ref[...])
    fq = proj(OFF_FQ, FOX_W) * (QK_SCALE * LOG2E)
    fk = proj(OFF_FK, FOX_W)
    fv = proj(OFF_FV, FOX_W)
    tri = tri_ref[...]
    c = carry_ref[...]
    for part in _split3(lf):
        c = c + jnp.dot(tri, part.astype(BF16), preferred_element_type=F32)
    carry_ref[...] = c[tm - 1:tm, :]
    c_parts = _split3(c * LOG2E)

    for hd in range(FOX_HEADS):
        lo = LANES * (hd // 2)

        def head_chunk(a):
            ch = a[:, lo:lo + LANES]
            if hd % 2:
                ch = pltpu.roll(ch, HEAD_DIM, 1)
            return jnp.where(lane < HEAD_DIM, ch, 0.0)

        b0, b1, b2 = (p[:, hd:hd + 1] for p in c_parts)
        q = head_chunk(fq)
        q = jnp.where(lane == X_LANE, b0, q)
        q = jnp.where(lane == X_LANE + 1, b1, q)
        q = jnp.where(lane == X_LANE + 2, b2, q)
        q = jnp.where((lane >= X_LANE + 3) & (lane < X_LANE + 6), 1.0, q)
        k = head_chunk(fk)
        k = jnp.where((lane >= X_LANE) & (lane < X_LANE + 3), 1.0, k)
        k = jnp.where(lane == X_LANE + 3, -b0, k)
        k = jnp.where(lane == X_LANE + 4, -b1, k)
        k = jnp.where(lane == X_LANE + 5, -b2, k)
        v = jnp.where(lane == X_LANE, 1.0, head_chunk(fv))
        qp_ref[hd] = q.T.astype(BF16)
        kp_ref[hd] = k.astype(BF16)
        vp_ref[hd] = v.T[:VT_ROWS, :].astype(BF16)

    fz_ref[...] = proj(OFF_FZ, FOX_W)
    rq_ref[...] = proj(OFF_RQ, RET_W)
    rk_ref[...] = proj(OFF_RK, RET_W)
    rv_ref[...] = proj(OFF_RV, RET_W).astype(BF16)
    rz_ref[...] = proj(OFF_RZ, RET_W)
    mq_ref[...] = (proj(OFF_MQ, MEM_W) * QK_SCALE).astype(BF16)
    mz_ref[...] = proj(OFF_MZ, MEM_W)


def _proj_call(x, g_pre, w_parts, layer, b_f_p, tri, tm):
    S = x.shape[0]
    w_spec = lambda w: pl.BlockSpec((1,) + w.shape[1:], lambda i: (layer, 0, 0))
    row = lambda w: pl.BlockSpec((tm, w), lambda i: (i, 0))
    const = lambda shape: pl.BlockSpec(shape, lambda i: (0,) * len(shape))
    head = pl.BlockSpec((FOX_HEADS, tm, LANES), lambda i: (0, i, 0))
    head_t = pl.BlockSpec((FOX_HEADS, LANES, tm), lambda i: (0, 0, i))
    head_vt = pl.BlockSpec((FOX_HEADS, VT_ROWS, tm), lambda i: (0, 0, i))
    out_shape = (
        jax.ShapeDtypeStruct((FOX_HEADS, LANES, S), BF16),
        jax.ShapeDtypeStruct((FOX_HEADS, S, LANES), BF16),
        jax.ShapeDtypeStruct((FOX_HEADS, VT_ROWS, S), BF16),
        jax.ShapeDtypeStruct((S, FOX_W), F32),
        jax.ShapeDtypeStruct((S, RET_W), F32),
        jax.ShapeDtypeStruct((S, RET_W), F32),
        jax.ShapeDtypeStruct((S, RET_W), BF16),
        jax.ShapeDtypeStruct((S, RET_W), F32),
        jax.ShapeDtypeStruct((S, MEM_W), BF16),
        jax.ShapeDtypeStruct((S, MEM_W), F32),
    )
    return pl.pallas_call(
        _proj_kernel,
        grid=(S // tm,),
        in_specs=[row(D_MODEL), const((1, D_MODEL))] + [w_spec(w) for w in w_parts]
                 + [const((1, LANES)), const((tm, tm))],
        out_specs=(head_t, head, head_vt, row(FOX_W), row(RET_W), row(RET_W), row(RET_W),
                   row(RET_W), row(MEM_W), row(MEM_W)),
        out_shape=out_shape,
        scratch_shapes=[pltpu.VMEM((1, LANES), F32)],
        compiler_params=pltpu.CompilerParams(
            dimension_semantics=("arbitrary",), vmem_limit_bytes=VMEM_LIMIT),
        name="proj",
    )(x, g_pre, *w_parts, b_f_p, tri)


def _max_rows(s):
    rows = s.shape[0]
    while rows % 16 == 0:
        rows //= 2
        s = jnp.maximum(s[:rows, :], s[rows:, :])
    return jnp.max(s, axis=0, keepdims=True)


def _fox_kernel(qt_ref, qtn_ref, kblk_ref, vtblk_ref, o_ref, m_ref, acc_ref, sa_ref, sb_ref, sc_ref,
                k_ref, vt_ref, *, bq, bk, sub):
    assert bk == bq
    i = pl.program_id(1)
    nsub = bq // sub
    chains = [(hh, sb) for hh in range(2) for sb in range(nsub)]

    m_ref[...] = jnp.full(m_ref.shape, NEG_BIG, F32)
    acc_ref[...] = jnp.zeros(acc_ref.shape, F32)

    new = pl.ds(pl.multiple_of(i * bk, bk), bk)
    k_ref[:, new, :] = kblk_ref[...]
    vt_ref[:, :, new] = vtblk_ref[...]

    def scores(c, q_ref, kb, nk=bk):
        hh, sb = c
        k = k_ref[hh, pl.ds(pl.multiple_of(kb * bk, bk), nk), :]
        return jnp.dot(k, q_ref[hh, :, sb * sub:(sb + 1) * sub], preferred_element_type=F32)

    def finish(c, s_ref, kb, masked):
        hh, sb = c
        cols = slice(sb * sub, (sb + 1) * sub)
        nk = (sb + 1) * sub if masked else bk
        s = s_ref[:nk, :]
        if masked:
            key = lax.broadcasted_iota(jnp.int32, (nk, sub), 0)
            qry = lax.broadcasted_iota(jnp.int32, (nk, sub), 1) + sb * sub
            s = jnp.where(key <= qry, s, NEG_BIG)
        vt = vt_ref[hh, :, pl.ds(pl.multiple_of(kb * bk, bk), nk)]
        m = m_ref[hh, :, cols]
        m_new = jnp.maximum(m, _max_rows(s))
        p = jnp.exp2(s - m_new)
        alpha = jnp.exp2(m - m_new)
        pv = jnp.dot(vt, p.astype(BF16), preferred_element_type=F32)
        acc_ref[hh, :, cols] = alpha * acc_ref[hh, :, cols] + pv
        m_ref[hh, :, cols] = m_new

    def step(kb, cur, nxt, next_is_diagonal=False):
        for n, c in enumerate(chains):
            nk = (c[1] + 1) * sub if next_is_diagonal else bk
            nxt[n, :nk, :] = scores(c, qt_ref, kb + 1, nk)
            finish(c, cur.at[n], kb, False)

    def last_step(cur, nxt):
        for n, c in enumerate(chains):
            nxt[n] = scores(c, qtn_ref, 0)
            finish(c, cur.at[n], i, True)

    @pl.when(i == 0)
    def _():
        for n, c in enumerate(chains):
            sa_ref[n] = scores(c, qt_ref, 0)

    def block(even_buf, odd_buf, handover, odd_block):
        def body(j, carry):
            step(2 * j, even_buf, odd_buf)
            step(2 * j + 1, odd_buf, even_buf)
            return carry

        lax.fori_loop(0, i // 2, body, 0)
        if odd_block:
            step(i - 1, even_buf, odd_buf, next_is_diagonal=True)
            last_step(odd_buf, handover)
        else:
            last_step(even_buf, handover)

    @pl.when(i % 2 == 0)
    def _():
        block(sa_ref, sb_ref, sc_ref, False)

    @pl.when(i % 2 == 1)
    def _():
        block(sc_ref, sb_ref, sa_ref, True)

    outs = [acc_ref[hh, :HEAD_DIM, :] / acc_ref[hh, X_LANE:X_LANE + 1, :] for hh in range(2)]
    o_ref[...] = jnp.concatenate(outs, axis=0).T


def _fox_call(qt, kp, vt, bq, bk, sub):
    _, S, _ = kp.shape
    nq = S // bq
    score_buf = pltpu.VMEM((2 * bq // sub, bk, sub), F32)
    return pl.pallas_call(
        functools.partial(_fox_kernel, bq=bq, bk=bk, sub=sub),
        grid=(FOX_HEADS // 2, S // bq),
        in_specs=[pl.BlockSpec((2, LANES, bq), lambda p, i: (p, 0, i)),
                  pl.BlockSpec((2, LANES, bq), lambda p, i: (p, 0, jnp.minimum(i + 1, nq - 1))),
                  pl.BlockSpec((2, bk, LANES), lambda p, i: (p, i, 0)),
                  pl.BlockSpec((2, VT_ROWS, bk), lambda p, i: (p, 0, i))],
        out_specs=pl.BlockSpec((bq, LANES), lambda p, i: (i, p)),
        out_shape=jax.ShapeDtypeStruct((S, FOX_W), F32),
        scratch_shapes=[pltpu.VMEM((2, 1, bq), F32), pltpu.VMEM((2, VT_ROWS, bq), F32),
                        score_buf, score_buf, score_buf,
                        pltpu.VMEM((2, S, LANES), BF16), pltpu.VMEM((2, VT_ROWS, S), BF16)],
        compiler_params=pltpu.CompilerParams(
            dimension_semantics=("arbitrary", "arbitrary"), vmem_limit_bytes=VMEM_LIMIT),
        name="fox",
    )(qt, qt, kp, vt)


def _ret_kernel(q_ref, k_ref, v_ref, cos_ref, sin_ref, dintra_ref, xi_ref, zeta_ref, g_ref,
                o_ref, r_ref, *, tb):
    j = pl.program_id(1)

    @pl.when(j == 0)
    def _():
        r_ref[...] = jnp.zeros_like(r_ref)

    C = RET_CHUNK
    half = HEAD_DIM // 2
    lane = lax.broadcasted_iota(jnp.int32, (tb, LANES), 1)
    first_half = (lane % HEAD_DIM) < half
    cos = cos_ref[...]
    sin = sin_ref[...]

    def rotary(x):
        swapped = jnp.where(first_half, pltpu.roll(x, LANES - half, 1), pltpu.roll(x, half, 1))
        return x * cos + swapped * sin

    q = rotary(q_ref[...])
    k = rotary(k_ref[...]) * QK_SCALE
    v = v_ref[...]
    xi = xi_ref[0]
    zeta = zeta_ref[0]
    g = g_ref[0]
    lane_c = lax.broadcasted_iota(jnp.int32, (C, LANES), 1)
    head0 = lane_c < HEAD_DIM
    same_head = (lax.broadcasted_iota(jnp.int32, (LANES, LANES), 0) // HEAD_DIM
                 == lax.broadcasted_iota(jnp.int32, (LANES, LANES), 1) // HEAD_DIM)
    nt = (((1,), (1,)), ((), ()))
    tn = (((0,), (0,)), ((), ()))

    chunks = range(tb // C)
    qc = [q[n * C:(n + 1) * C, :] for n in chunks]
    kc = [k[n * C:(n + 1) * C, :] for n in chunks]
    vc = [v[n * C:(n + 1) * C, :] for n in chunks]
    a = [[lax.dot_general(jnp.where(head0 if hh == 0 else ~head0, qc[n], 0.0).astype(BF16),
                          kc[n].astype(BF16), nt, preferred_element_type=F32)
          for hh in range(2)] for n in chunks]
    kv = [lax.dot_general((kc[n] * zeta).astype(BF16), vc[n], tn, preferred_element_type=F32)
          for n in chunks]
    o_intra = [[jnp.dot((a[n][hh] * dintra_ref[0, hh]).astype(BF16), vc[n],
                        preferred_element_type=F32)
                for hh in range(2)] for n in chunks]
    R = r_ref[...]
    for n in chunks:
        o = jnp.where(head0, o_intra[n][0], o_intra[n][1])
        o = o + jnp.dot((qc[n] * xi).astype(BF16), R.astype(BF16), preferred_element_type=F32)
        R = g * R + jnp.where(same_head, kv[n], 0.0)
        sq = o * o
        s0 = jnp.sum(jnp.where(head0, sq, 0.0), axis=1, keepdims=True)
        s1 = jnp.sum(jnp.where(head0, 0.0, sq), axis=1, keepdims=True)
        ms = jnp.where(head0, s0, s1) * (1.0 / HEAD_DIM)
        o_ref[n * C:(n + 1) * C, :] = o * lax.rsqrt(ms + EPS)
    r_ref[...] = R


def _ret_call(rq, rk, rv, tabs, tb):
    S = rq.shape[0]
    cos, sin, dintra, xi, zeta, g = tabs
    row = pl.BlockSpec((tb, LANES), lambda p, j: (j, p))
    tab = pl.BlockSpec((tb, LANES), lambda p, j: (j, 0))
    pair = lambda shape: pl.BlockSpec((1,) + shape, lambda p, j: (p,) + (0,) * len(shape))
    return pl.pallas_call(
        functools.partial(_ret_kernel, tb=tb),
        grid=(RET_HEADS // 2, S // tb),
        in_specs=[row, row, row, tab, tab,
                  pair((2, RET_CHUNK, RET_CHUNK)), pair((RET_CHUNK, LANES)),
                  pair((RET_CHUNK, LANES)), pair((LANES, LANES))],
        out_specs=row,
        out_shape=jax.ShapeDtypeStruct((S, RET_W), F32),
        scratch_shapes=[pltpu.VMEM((LANES, LANES), F32)],
        compiler_params=pltpu.CompilerParams(
            dimension_semantics=("arbitrary", "arbitrary"), vmem_limit_bytes=VMEM_LIMIT),
        name="ret",
    )(rq, rk, rv, cos, sin, dintra, xi, zeta, g)


def _ret_tables(S):
    half = HEAD_DIM // 2
    C = RET_CHUNK
    pos = jnp.arange(S, dtype=F32)
    freqs = ROPE_BASE ** (-jnp.arange(half, dtype=F32) / half)
    ang = pos[:, None] * freqs[None, :]
    cos, sin = jnp.cos(ang), jnp.sin(ang)
    cos_t = jnp.tile(jnp.concatenate([cos, cos], axis=1), (1, 2))
    sin_t = jnp.tile(jnp.concatenate([-sin, sin], axis=1), (1, 2))
    log_gamma = jnp.log(1.0 - 2.0 ** (-5.0 - jnp.arange(RET_HEADS, dtype=F32)))
    idx = jnp.arange(C, dtype=F32)
    diff = idx[:, None] - idx[None, :]
    decay_intra = jnp.where(diff >= 0, jnp.exp(log_gamma[:, None, None] * jnp.maximum(diff, 0.0)), 0.0)
    zeta = jnp.exp(log_gamma[:, None] * (C - 1 - idx)[None, :])
    xi = jnp.exp(log_gamma[:, None] * (idx + 1)[None, :])
    decay_chunk = jnp.exp(log_gamma * C)
    npair = RET_HEADS // 2
    dintra = decay_intra.reshape(npair, 2, C, C)
    lanes = lambda t: jnp.repeat(t.reshape(npair, 2, C).transpose(0, 2, 1), HEAD_DIM, axis=2)
    gd = jnp.repeat(decay_chunk.reshape(npair, 2), HEAD_DIM, axis=1)
    blk = jnp.arange(LANES) // HEAD_DIM
    g = jnp.where(blk[:, None] == blk[None, :], gd[:, :, None], 0.0)
    return cos_t, sin_t, dintra, lanes(xi), lanes(zeta), g


def _memkv_kernel(mem_ref, g_ref, w_ref, mk_ref, mv_ref):
    x = mem_ref[...]
    ms = jnp.mean(x * x, axis=-1, keepdims=True)
    h = (x * lax.rsqrt(ms + EPS) * g_ref[0]).astype(BF16)
    kv = jnp.dot(h, w_ref[0], preferred_element_type=F32)
    mk_ref[0] = kv[:, :MEM_W].astype(BF16)
    mv_ref[0] = kv[:, MEM_W:].astype(BF16)


def _memkv_call(mem, g_mem, w_mkv):
    M = mem.shape[0]
    return pl.pallas_call(
        _memkv_kernel,
        grid=(DEPTH,),
        in_specs=[pl.BlockSpec((M, D_MODEL), lambda l: (0, 0)),
                  pl.BlockSpec((1, 1, D_MODEL), lambda l: (l, 0, 0)),
                  pl.BlockSpec((1, D_MODEL, 2 * MEM_W), lambda l: (l, 0, 0))],
        out_specs=(pl.BlockSpec((1, M, MEM_W), lambda l: (l, 0, 0)),
                   pl.BlockSpec((1, M, MEM_W), lambda l: (l, 0, 0))),
        out_shape=(jax.ShapeDtypeStruct((DEPTH, M, MEM_W), BF16),
                   jax.ShapeDtypeStruct((DEPTH, M, MEM_W), BF16)),
        compiler_params=pltpu.CompilerParams(
            dimension_semantics=("arbitrary",), vmem_limit_bytes=VMEM_LIMIT),
        name="memkv",
    )(mem, g_mem, w_mkv)


def _out_kernel(x_ref, fox_ref, fz_ref, ret_ref, rz_ref, mq_ref, mz_ref, mk_ref, mv_ref,
                w_ref, g_ref, o_ref):
    tm = x_ref.shape[0]
    lane = lax.broadcasted_iota(jnp.int32, (tm, LANES), 1)
    head0 = lane < HEAD_DIM
    nt = (((1,), (1,)), ((), ()))

    pairs = range(MEM_HEADS // 2)
    s = []
    for pr in pairs:
        cols = slice(pr * LANES, (pr + 1) * LANES)
        q2 = mq_ref[:, cols]
        for hh in range(2):
            qm = jnp.where(head0 if hh == 0 else ~head0, q2, jnp.zeros_like(q2))
            s.append(lax.dot_general(qm, mk_ref[0, :, cols], nt, preferred_element_type=F32))

    y_fr = jnp.concatenate([
        (fox_ref[...] * _silu(fz_ref[...])).astype(BF16),
        (ret_ref[...] * _silu(rz_ref[...])).astype(BF16),
    ], axis=1)
    o = jnp.dot(y_fr, w_ref[0, :FOX_W + RET_W, :], preferred_element_type=F32)

    mxa = []
    for pr in pairs:
        cols = slice(pr * LANES, (pr + 1) * LANES)
        o_h = []
        for hh in range(2):
            sc = s[2 * pr + hh]
            e = jnp.exp(sc - jnp.max(sc, axis=1, keepdims=True))
            p = e / jnp.sum(e, axis=1, keepdims=True)
            o_h.append(jnp.dot(p.astype(BF16), mv_ref[0, :, cols], preferred_element_type=F32))
        mxa.append(jnp.where(head0, o_h[0], o_h[1]))
    y_m = (jnp.concatenate(mxa, axis=1) * _silu(mz_ref[...])).astype(BF16)
    o = o + jnp.dot(y_m, w_ref[0, FOX_W + RET_W:, :], preferred_element_type=F32)
    ms = jnp.mean(o * o, axis=-1, keepdims=True)
    o_ref[...] = x_ref[...] + o * lax.rsqrt(ms + EPS) * g_ref[...]


def _out_call(x, fox, fz, ret, rz, mq, mz, mk, mv, layer, w_out, g_post, tm):
    S = x.shape[0]
    M = mk.shape[1]
    row = lambda w: pl.BlockSpec((tm, w), lambda i: (i, 0))
    const = lambda shape: pl.BlockSpec(shape, lambda i: (0,) * len(shape))
    mem_spec = pl.BlockSpec((1, M, MEM_W), lambda i: (layer, 0, 0))
    return pl.pallas_call(
        _out_kernel,
        grid=(S // tm,),
        in_specs=[row(D_MODEL), row(FOX_W), row(FOX_W), row(RET_W), row(RET_W), row(MEM_W),
                  row(MEM_W), mem_spec, mem_spec,
                  pl.BlockSpec((1, D_MIX, D_MODEL), lambda i: (layer, 0, 0)), const((1, D_MODEL))],
        out_specs=row(D_MODEL),
        out_shape=jax.ShapeDtypeStruct((S, D_MODEL), F32),
        compiler_params=pltpu.CompilerParams(
            dimension_semantics=("arbitrary",), vmem_limit_bytes=VMEM_LIMIT),
        name="out",
    )(x, fox, fz, ret, rz, mq, mz, mk, mv, w_out, g_post)


def _tiles(S):
    tm = min(512, S)
    bq = min(1024, S)
    bk = bq
    sub = min(256, bq)
    tb = min(1024, S)
    return tm, bq, bk, sub, tb


def _split_w_in(w_in):
    ff0 = 4 * FOX_W
    ff = jnp.pad(w_in[:, :, ff0:ff0 + FOX_HEADS], ((0, 0), (0, 0), (0, LANES - FOX_HEADS)))
    return (w_in[:, :, :ff0].astype(BF16), w_in[:, :, ff0 + FOX_HEADS:].astype(BF16),
            ff.astype(BF16))


def kernel(x, mem, w_in, b_f, w_out, w_mem_kv, pre_norm, post_norm, mem_norm):
    B, S, _ = x.shape
    assert B == 1
    tm, bq, bk, sub, tb = _tiles(S)
    w_in_p = _split_w_in(w_in)
    b_f_p = jnp.pad(b_f, ((0, 0), (0, LANES - FOX_HEADS)))
    tri = jnp.tril(jnp.ones((tm, tm), F32)).astype(BF16)
    tabs = _ret_tables(S)
    mk, mv = _memkv_call(mem[0], mem_norm[:, None, :], w_mem_kv.astype(BF16))
    w_out_b = w_out.astype(BF16)

    xs = x[0]
    for l in range(DEPTH):
        qp, kp, vp, fz, rq, rk, rv, rz, mq, mz = _proj_call(
            xs, pre_norm[l][None, :], w_in_p, l, b_f_p[l][None, :], tri, tm)
        fox = _fox_call(qp, kp, vp, bq, bk, sub)
        ret = _ret_call(rq, rk, rv, tabs, tb)
        xs = _out_call(xs, fox, fz, ret, rz, mq, mz, mk, mv, l, w_out_b,
                       post_norm[l][None, :], 2 * tm)
    return xs[None]
```

```python
import functools

import jax
import jax.numpy as jnp
from jax import lax
from jax.experimental import pallas as pl
from jax.experimental.pallas import tpu as pltpu

F32 = jnp.float32
BF16 = jnp.bfloat16

D_MODEL = 1024
DEPTH = 4
N_MEM = 256
HEAD_DIM = 64
FOX_HEADS = 8
RET_HEADS = 4
MEM_HEADS = 4
FOX_W = FOX_HEADS * HEAD_DIM
RET_W = RET_HEADS * HEAD_DIM
MEM_W = MEM_HEADS * HEAD_DIM
D_MIX = FOX_W + RET_W + MEM_W
RET_CHUNK = 128
ROPE_BASE = 10000.0
EPS = 1e-6
QK_SCALE = HEAD_DIM ** -0.5

LANES = 128
SUBLANES = 8
LOG2E = 1.4426950408889634
NEG_BIG = -1e30

OFF_FQ = 0
OFF_FK = OFF_FQ + FOX_W
OFF_FV = OFF_FK + FOX_W
OFF_FZ = OFF_FV + FOX_W
OFF_RQ = OFF_FZ + FOX_W
OFF_RK = OFF_RQ + RET_W
OFF_RV = OFF_RK + RET_W
OFF_RZ = OFF_RV + RET_W
OFF_MQ = OFF_RZ + RET_W
OFF_MZ = OFF_MQ + MEM_W
OFF_FF = OFF_MZ + MEM_W

X_LANE = HEAD_DIM
VT_ROWS = 80

VMEM_LIMIT = 56 * 1024 * 1024


def _split3(x):
    hi = x.astype(BF16).astype(F32)
    r1 = x - hi
    mid = r1.astype(BF16).astype(F32)
    lo = (r1 - mid).astype(BF16).astype(F32)
    return hi, mid, lo


def _log_sigmoid(x):
    return -(jnp.maximum(-x, 0.0) + jnp.log1p(jnp.exp(-jnp.abs(x))))


def _silu(z):
    return z * (1.0 / (1.0 + jnp.exp(-z)))


def _proj_kernel(x_ref, g_ref, wh_ref, wt_ref, wf_ref, bf_ref, tri_ref,
                 qp_ref, kp_ref, vp_ref, fz_ref, rq_ref, rk_ref, rv_ref, rz_ref,
                 mq_ref, mz_ref, carry_ref):
    i = pl.program_id(0)

    @pl.when(i == 0)
    def _():
        carry_ref[...] = jnp.zeros_like(carry_ref)

    x = x_ref[...]
    ms = jnp.mean(x * x, axis=-1, keepdims=True)
    h = (x * lax.rsqrt(ms + EPS) * g_ref[...]).astype(BF16)

    def proj(off, width):
        if off < OFF_RQ:
            w = wh_ref[0, :, off:off + width]
        elif off < OFF_FF:
            w = wt_ref[0, :, off - OFF_RQ:off - OFF_RQ + width]
        else:
            w = wf_ref[0]
        return jnp.dot(h, w, preferred_element_type=F32)

    tm = x.shape[0]
    lane = lax.broadcasted_iota(jnp.int32, (tm, LANES), 1)

    lf = _log_sigmoid(proj(OFF_FF, LANES) + bf_ref[...])
    fq = proj(OFF_FQ, FOX_W) * (QK_SCALE * LOG2E)
    fk = proj(OFF_FK, FOX_W)
    fv = proj(OFF_FV, FOX_W)
    tri = tri_ref[...]
    c = carry_ref[...]
    for part in _split3(lf):
        c = c + jnp.dot(tri, part.astype(BF16), preferred_element_type=F32)
    carry_ref[...] = c[tm - 1:tm, :]
    c_parts = _split3(c * LOG2E)

    for hd in range(FOX_HEADS):
        lo = LANES * (hd // 2)

        def head_chunk(a):
            ch = a[:, lo:lo + LANES]
            if hd % 2:
                ch = pltpu.roll(ch, HEAD_DIM, 1)
            return jnp.where(lane < HEAD_DIM, ch, 0.0)

        b0, b1, b2 = (p[:, hd:hd + 1] for p in c_parts)
        q = head_chunk(fq)
        q = jnp.where(lane == X_LANE, b0, q)
        q = jnp.where(lane == X_LANE + 1, b1, q)
        q = jnp.where(lane == X_LANE + 2, b2, q)
        q = jnp.where((lane >= X_LANE + 3) & (lane < X_LANE + 6), 1.0, q)
        k = head_chunk(fk)
        k = jnp.where((lane >= X_LANE) & (lane < X_LANE + 3), 1.0, k)
        k = jnp.where(lane == X_LANE + 3, -b0, k)
        k = jnp.where(lane == X_LANE + 4, -b1, k)
        k = jnp.where(lane == X_LANE + 5, -b2, k)
        v = jnp.where(lane == X_LANE, 1.0, head_chunk(fv))
        qp_ref[hd] = q.T.astype(BF16)
        kp_ref[hd] = k.astype(BF16)
        vp_ref[hd] = v.T[:VT_ROWS, :].astype(BF16)

    fz_ref[...] = proj(OFF_FZ, FOX_W)
    rq_ref[...] = proj(OFF_RQ, RET_W)
    rk_ref[...] = proj(OFF_RK, RET_W)
    rv_ref[...] = proj(OFF_RV, RET_W).astype(BF16)
    rz_ref[...] = proj(OFF_RZ, RET_W)
    mq_ref[...] = (proj(OFF_MQ, MEM_W) * QK_SCALE).astype(BF16)
    mz_ref[...] = proj(OFF_MZ, MEM_W)


def _proj_call(x, g_pre, w_parts, layer, b_f_p, tri, tm):
    S = x.shape[0]
    w_spec = lambda w: pl.BlockSpec((1,) + w.shape[1:], lambda i: (layer, 0, 0))
    row = lambda w: pl.BlockSpec((tm, w), lambda i: (i, 0))
    const = lambda shape: pl.BlockSpec(shape, lambda i: (0,) * len(shape))
    head = pl.BlockSpec((FOX_HEADS, tm, LANES), lambda i: (0, i, 0))
    head_t = pl.BlockSpec((FOX_HEADS, LANES, tm), lambda i: (0, 0, i))
    head_vt = pl.BlockSpec((FOX_HEADS, VT_ROWS, tm), lambda i: (0, 0, i))
    out_shape = (
        jax.ShapeDtypeStruct((FOX_HEADS, LANES, S), BF16),
        jax.ShapeDtypeStruct((FOX_HEADS, S, LANES), BF16),
        jax.ShapeDtypeStruct((FOX_HEADS, VT_ROWS, S), BF16),
        jax.ShapeDtypeStruct((S, FOX_W), F32),
        jax.ShapeDtypeStruct((S, RET_W), F32),
        jax.ShapeDtypeStruct((S, RET_W), F32),
        jax.ShapeDtypeStruct((S, RET_W), BF16),
        jax.ShapeDtypeStruct((S, RET_W), F32),
        jax.ShapeDtypeStruct((S, MEM_W), BF16),
        jax.ShapeDtypeStruct((S, MEM_W), F32),
    )
    return pl.pallas_call(
        _proj_kernel,
        grid=(S // tm,),
        in_specs=[row(D_MODEL), const((1, D_MODEL))] + [w_spec(w) for w in w_parts]
                 + [const((1, LANES)), const((tm, tm))],
        out_specs=(head_t, head, head_vt, row(FOX_W), row(RET_W), row(RET_W), row(RET_W),
                   row(RET_W), row(MEM_W), row(MEM_W)),
        out_shape=out_shape,
        scratch_shapes=[pltpu.VMEM((1, LANES), F32)],
        compiler_params=pltpu.CompilerParams(
            dimension_semantics=("arbitrary",), vmem_limit_bytes=VMEM_LIMIT),
        name="proj",
    )(x, g_pre, *w_parts, b_f_p, tri)


def _max_rows(s):
    rows = s.shape[0]
    while rows % (2 * SUBLANES) == 0:
        rows //= 2
        s = jnp.maximum(s[:rows, :], s[rows:, :])
    return jnp.max(s, axis=0, keepdims=True)


def _fox_kernel(qt_ref, qtn_ref, kblk_ref, vtblk_ref, o_ref, m_ref, acc_ref, sa_ref, sb_ref, sc_ref,
                k_ref, vt_ref, *, bq, bk, sub):
    assert bk == bq
    i = pl.program_id(1)
    nsub = bq // sub
    chains = [(hh, sb) for hh in range(2) for sb in range(nsub)]

    m_ref[...] = jnp.full(m_ref.shape, NEG_BIG, F32)
    acc_ref[...] = jnp.zeros(acc_ref.shape, F32)

    new = pl.ds(pl.multiple_of(i * bk, bk), bk)
    k_ref[:, new, :] = kblk_ref[...]
    vt_ref[:, :, new] = vtblk_ref[...]

    def scores(c, q_ref, kb, nk=bk):
        hh, sb = c
        k = k_ref[hh, pl.ds(pl.multiple_of(kb * bk, bk), nk), :]
        return jnp.dot(k, q_ref[hh, :, sb * sub:(sb + 1) * sub], preferred_element_type=F32)

    def finish(c, s_ref, kb, masked):
        hh, sb = c
        cols = slice(sb * sub, (sb + 1) * sub)
        nk = (sb + 1) * sub if masked else bk
        s = s_ref[:nk, :]
        if masked:
            key = lax.broadcasted_iota(jnp.int32, (nk, sub), 0)
            qry = lax.broadcasted_iota(jnp.int32, (nk, sub), 1) + sb * sub
            s = jnp.where(key <= qry, s, NEG_BIG)
        vt = vt_ref[hh, :, pl.ds(pl.multiple_of(kb * bk, bk), nk)]
        m = m_ref[hh, :, cols]
        m_new = jnp.maximum(m, _max_rows(s))
        p = jnp.exp2(s - m_new)
        alpha = jnp.exp2(m - m_new)
        pv = jnp.dot(vt, p.astype(BF16), preferred_element_type=F32)
        acc_ref[hh, :, cols] = alpha * acc_ref[hh, :, cols] + pv
        m_ref[hh, :, cols] = m_new

    def step(kb, cur, nxt, next_is_diagonal=False):
        for n, c in enumerate(chains):
            nk = (c[1] + 1) * sub if next_is_diagonal else bk
            nxt[n, :nk, :] = scores(c, qt_ref, kb + 1, nk)
            finish(c, cur.at[n], kb, False)

    def last_step(cur, nxt):
        for n, c in enumerate(chains):
            nxt[n] = scores(c, qtn_ref, 0)
            finish(c, cur.at[n], i, True)

    @pl.when(i == 0)
    def _():
        for n, c in enumerate(chains):
            sa_ref[n] = scores(c, qt_ref, 0)

    def block(even_buf, odd_buf, handover, odd_block):
        def body(j, carry):
            step(2 * j, even_buf, odd_buf)
            step(2 * j + 1, odd_buf, even_buf)
            return carry

        if odd_block:
            lax.fori_loop(0, i // 2, body, 0)
            step(i - 1, even_buf, odd_buf, next_is_diagonal=True)
            last_step(odd_buf, handover)
        else:
            lax.fori_loop(0, jnp.maximum(i // 2 - 1, 0), body, 0)

            @pl.when(i >= 2)
            def _():
                step(i - 2, even_buf, odd_buf)
                step(i - 1, odd_buf, even_buf, next_is_diagonal=True)

            last_step(even_buf, handover)

    @pl.when(i % 2 == 0)
    def _():
        block(sa_ref, sb_ref, sc_ref, False)

    @pl.when(i % 2 == 1)
    def _():
        block(sc_ref, sb_ref, sa_ref, True)

    outs = [acc_ref[hh, :HEAD_DIM, :] / acc_ref[hh, X_LANE:X_LANE + 1, :] for hh in range(2)]
    o_ref[...] = jnp.concatenate(outs, axis=0).T


def _fox_call(qt, kp, vt, bq, bk, sub):
    _, S, _ = kp.shape
    nq = S // bq
    score_buf = pltpu.VMEM((2 * bq // sub, bk, sub), F32)
    return pl.pallas_call(
        functools.partial(_fox_kernel, bq=bq, bk=bk, sub=sub),
        grid=(FOX_HEADS // 2, S // bq),
        in_specs=[pl.BlockSpec((2, LANES, bq), lambda p, i: (p, 0, i)),
                  pl.BlockSpec((2, LANES, bq), lambda p, i: (p, 0, jnp.minimum(i + 1, nq - 1))),
                  pl.BlockSpec((2, bk, LANES), lambda p, i: (p, i, 0)),
                  pl.BlockSpec((2, VT_ROWS, bk), lambda p, i: (p, 0, i))],
        out_specs=pl.BlockSpec((bq, LANES), lambda p, i: (i, p)),
        out_shape=jax.ShapeDtypeStruct((S, FOX_W), F32),
        scratch_shapes=[pltpu.VMEM((2, 1, bq), F32), pltpu.VMEM((2, VT_ROWS, bq), F32),
                        score_buf, score_buf, score_buf,
                        pltpu.VMEM((2, S, LANES), BF16), pltpu.VMEM((2, VT_ROWS, S), BF16)],
        compiler_params=pltpu.CompilerParams(
            dimension_semantics=("arbitrary", "arbitrary"), vmem_limit_bytes=VMEM_LIMIT),
        name="fox",
    )(qt, qt, kp, vt)


def _ret_kernel(q_ref, k_ref, v_ref, cos_ref, sin_ref, dintra_ref, xi_ref, zeta_ref, g_ref,
                o_ref, r_ref, *, tb):
    j = pl.program_id(1)

    @pl.when(j == 0)
    def _():
        r_ref[...] = jnp.zeros_like(r_ref)

    C = RET_CHUNK
    half = HEAD_DIM // 2
    lane = lax.broadcasted_iota(jnp.int32, (tb, LANES), 1)
    first_half = (lane % HEAD_DIM) < half
    cos = cos_ref[...]
    sin = sin_ref[...]

    def rotary(x):
        swapped = jnp.where(first_half, pltpu.roll(x, LANES - half, 1), pltpu.roll(x, half, 1))
        return x * cos + swapped * sin

    q = rotary(q_ref[...])
    k = rotary(k_ref[...]) * QK_SCALE
    v = v_ref[...]
    xi = xi_ref[0]
    zeta = zeta_ref[0]
    g = g_ref[0]
    lane_c = lax.broadcasted_iota(jnp.int32, (C, LANES), 1)
    head0 = lane_c < HEAD_DIM
    same_head = (lax.broadcasted_iota(jnp.int32, (LANES, LANES), 0) // HEAD_DIM
                 == lax.broadcasted_iota(jnp.int32, (LANES, LANES), 1) // HEAD_DIM)
    nt = (((1,), (1,)), ((), ()))
    tn = (((0,), (0,)), ((), ()))

    chunks = range(tb // C)
    qc = [q[n * C:(n + 1) * C, :] for n in chunks]
    kc = [k[n * C:(n + 1) * C, :] for n in chunks]
    vc = [v[n * C:(n + 1) * C, :] for n in chunks]
    a = [[lax.dot_general(jnp.where(head0 if hh == 0 else ~head0, qc[n], 0.0).astype(BF16),
                          kc[n].astype(BF16), nt, preferred_element_type=F32)
          for hh in range(2)] for n in chunks]
    kv = [lax.dot_general((kc[n] * zeta).astype(BF16), vc[n], tn, preferred_element_type=F32)
          for n in chunks]
    o_intra = [[jnp.dot((a[n][hh] * dintra_ref[0, hh]).astype(BF16), vc[n],
                        preferred_element_type=F32)
                for hh in range(2)] for n in chunks]
    R = r_ref[...]
    for n in chunks:
        o = jnp.where(head0, o_intra[n][0], o_intra[n][1])
        o = o + jnp.dot((qc[n] * xi).astype(BF16), R.astype(BF16), preferred_element_type=F32)
        R = g * R + jnp.where(same_head, kv[n], 0.0)
        sq = o * o
        s0 = jnp.sum(jnp.where(head0, sq, 0.0), axis=1, keepdims=True)
        s1 = jnp.sum(jnp.where(head0, 0.0, sq), axis=1, keepdims=True)
        ms = jnp.where(head0, s0, s1) * (1.0 / HEAD_DIM)
        o_ref[n * C:(n + 1) * C, :] = o * lax.rsqrt(ms + EPS)
    r_ref[...] = R


def _ret_call(rq, rk, rv, tabs, tb):
    S = rq.shape[0]
    cos, sin, dintra, xi, zeta, g = tabs
    row = pl.BlockSpec((tb, LANES), lambda p, j: (j, p))
    tab = pl.BlockSpec((tb, LANES), lambda p, j: (j, 0))
    pair = lambda shape: pl.BlockSpec((1,) + shape, lambda p, j: (p,) + (0,) * len(shape))
    return pl.pallas_call(
        functools.partial(_ret_kernel, tb=tb),
        grid=(RET_HEADS // 2, S // tb),
        in_specs=[row, row, row, tab, tab,
                  pair((2, RET_CHUNK, RET_CHUNK)), pair((RET_CHUNK, LANES)),
                  pair((RET_CHUNK, LANES)), pair((LANES, LANES))],
        out_specs=row,
        out_shape=jax.ShapeDtypeStruct((S, RET_W), F32),
        scratch_shapes=[pltpu.VMEM((LANES, LANES), F32)],
        compiler_params=pltpu.CompilerParams(
            dimension_semantics=("arbitrary", "arbitrary"), vmem_limit_bytes=VMEM_LIMIT),
        name="ret",
    )(rq, rk, rv, cos, sin, dintra, xi, zeta, g)


def _ret_tables(S):
    half = HEAD_DIM // 2
    C = RET_CHUNK
    pos = jnp.arange(S, dtype=F32)
    freqs = ROPE_BASE ** (-jnp.arange(half, dtype=F32) / half)
    ang = pos[:, None] * freqs[None, :]
    cos, sin = jnp.cos(ang), jnp.sin(ang)
    cos_t = jnp.tile(jnp.concatenate([cos, cos], axis=1), (1, 2))
    sin_t = jnp.tile(jnp.concatenate([-sin, sin], axis=1), (1, 2))
    log_gamma = jnp.log(1.0 - 2.0 ** (-5.0 - jnp.arange(RET_HEADS, dtype=F32)))
    idx = jnp.arange(C, dtype=F32)
    diff = idx[:, None] - idx[None, :]
    decay_intra = jnp.where(diff >= 0, jnp.exp(log_gamma[:, None, None] * jnp.maximum(diff, 0.0)), 0.0)
    zeta = jnp.exp(log_gamma[:, None] * (C - 1 - idx)[None, :])
    xi = jnp.exp(log_gamma[:, None] * (idx + 1)[None, :])
    decay_chunk = jnp.exp(log_gamma * C)
    npair = RET_HEADS // 2
    dintra = decay_intra.reshape(npair, 2, C, C)
    lanes = lambda t: jnp.repeat(t.reshape(npair, 2, C).transpose(0, 2, 1), HEAD_DIM, axis=2)
    gd = jnp.repeat(decay_chunk.reshape(npair, 2), HEAD_DIM, axis=1)
    blk = jnp.arange(LANES) // HEAD_DIM
    g = jnp.where(blk[:, None] == blk[None, :], gd[:, :, None], 0.0)
    return cos_t, sin_t, dintra, lanes(xi), lanes(zeta), g


def _memkv_kernel(mem_ref, g_ref, w_ref, mk_ref, mv_ref):
    x = mem_ref[...]
    ms = jnp.mean(x * x, axis=-1, keepdims=True)
    h = (x * lax.rsqrt(ms + EPS) * g_ref[0]).astype(BF16)
    kv = jnp.dot(h, w_ref[0], preferred_element_type=F32)
    mk_ref[0] = kv[:, :MEM_W].astype(BF16)
    mv_ref[0] = kv[:, MEM_W:].astype(BF16)


def _memkv_call(mem, g_mem, w_mkv):
    M = mem.shape[0]
    return pl.pallas_call(
        _memkv_kernel,
        grid=(DEPTH,),
        in_specs=[pl.BlockSpec((M, D_MODEL), lambda l: (0, 0)),
                  pl.BlockSpec((1, 1, D_MODEL), lambda l: (l, 0, 0)),
                  pl.BlockSpec((1, D_MODEL, 2 * MEM_W), lambda l: (l, 0, 0))],
        out_specs=(pl.BlockSpec((1, M, MEM_W), lambda l: (l, 0, 0)),
                   pl.BlockSpec((1, M, MEM_W), lambda l: (l, 0, 0))),
        out_shape=(jax.ShapeDtypeStruct((DEPTH, M, MEM_W), BF16),
                   jax.ShapeDtypeStruct((DEPTH, M, MEM_W), BF16)),
        compiler_params=pltpu.CompilerParams(
            dimension_semantics=("arbitrary",), vmem_limit_bytes=VMEM_LIMIT),
        name="memkv",
    )(mem, g_mem, w_mkv)


def _out_kernel(x_ref, fox_ref, fz_ref, ret_ref, rz_ref, mq_ref, mz_ref, mk_ref, mv_ref,
                w_ref, g_ref, o_ref):
    tm = x_ref.shape[0]
    lane = lax.broadcasted_iota(jnp.int32, (tm, LANES), 1)
    head0 = lane < HEAD_DIM
    nt = (((1,), (1,)), ((), ()))

    pairs = range(MEM_HEADS // 2)
    s = []
    for pr in pairs:
        cols = slice(pr * LANES, (pr + 1) * LANES)
        q2 = mq_ref[:, cols]
        for hh in range(2):
            qm = jnp.where(head0 if hh == 0 else ~head0, q2, jnp.zeros_like(q2))
            s.append(lax.dot_general(qm, mk_ref[0, :, cols], nt, preferred_element_type=F32))

    y_fr = jnp.concatenate([
        (fox_ref[...] * _silu(fz_ref[...])).astype(BF16),
        (ret_ref[...] * _silu(rz_ref[...])).astype(BF16),
    ], axis=1)
    o = jnp.dot(y_fr, w_ref[0, :FOX_W + RET_W, :], preferred_element_type=F32)

    mxa = []
    for pr in pairs:
        cols = slice(pr * LANES, (pr + 1) * LANES)
        o_h = []
        for hh in range(2):
            sc = s[2 * pr + hh]
            e = jnp.exp(sc - jnp.max(sc, axis=1, keepdims=True))
            p = e / jnp.sum(e, axis=1, keepdims=True)
            o_h.append(jnp.dot(p.astype(BF16), mv_ref[0, :, cols], preferred_element_type=F32))
        mxa.append(jnp.where(head0, o_h[0], o_h[1]))
    y_m = (jnp.concatenate(mxa, axis=1) * _silu(mz_ref[...])).astype(BF16)
    o = o + jnp.dot(y_m, w_ref[0, FOX_W + RET_W:, :], preferred_element_type=F32)
    ms = jnp.mean(o * o, axis=-1, keepdims=True)
    o_ref[...] = x_ref[...] + o * lax.rsqrt(ms + EPS) * g_ref[...]


def _out_call(x, fox, fz, ret, rz, mq, mz, mk, mv, layer, w_out, g_post, tm):
    S = x.shape[0]
    M = mk.shape[1]
    row = lambda w: pl.BlockSpec((tm, w), lambda i: (i, 0))
    const = lambda shape: pl.BlockSpec(shape, lambda i: (0,) * len(shape))
    mem_spec = pl.BlockSpec((1, M, MEM_W), lambda i: (layer, 0, 0))
    return pl.pallas_call(
        _out_kernel,
        grid=(S // tm,),
        in_specs=[row(D_MODEL), row(FOX_W), row(FOX_W), row(RET_W), row(RET_W), row(MEM_W),
                  row(MEM_W), mem_spec, mem_spec,
                  pl.BlockSpec((1, D_MIX, D_MODEL), lambda i: (layer, 0, 0)), const((1, D_MODEL))],
        out_specs=row(D_MODEL),
        out_shape=jax.ShapeDtypeStruct((S, D_MODEL), F32),
        compiler_params=pltpu.CompilerParams(
            dimension_semantics=("arbitrary",), vmem_limit_bytes=VMEM_LIMIT),
        name="out",
    )(x, fox, fz, ret, rz, mq, mz, mk, mv, w_out, g_post)


def _tiles(S):
    tm = min(512, S)
    tm_out = min(1024, S)
    bq = min(1024, S)
    bk = bq
    sub = min(256, bq)
    tb = min(2048, S)
    return tm, tm_out, bq, bk, sub, tb


def _split_w_in(w_in):
    ff0 = 4 * FOX_W
    ff = jnp.pad(w_in[:, :, ff0:ff0 + FOX_HEADS], ((0, 0), (0, 0), (0, LANES - FOX_HEADS)))
    return (w_in[:, :, :ff0].astype(BF16), w_in[:, :, ff0 + FOX_HEADS:].astype(BF16),
            ff.astype(BF16))


def kernel(x, mem, w_in, b_f, w_out, w_mem_kv, pre_norm, post_norm, mem_norm):
    B, S, _ = x.shape
    assert B == 1
    tm, tm_out, bq, bk, sub, tb = _tiles(S)
    w_in_p = _split_w_in(w_in)
    b_f_p = jnp.pad(b_f, ((0, 0), (0, LANES - FOX_HEADS)))
    tri = jnp.tril(jnp.ones((tm, tm), F32)).astype(BF16)
    tabs = _ret_tables(S)
    mk, mv = _memkv_call(mem[0], mem_norm[:, None, :], w_mem_kv.astype(BF16))
    w_out_b = w_out.astype(BF16)

    xs = x[0]
    for l in range(DEPTH):
        qp, kp, vp, fz, rq, rk, rv, rz, mq, mz = _proj_call(
            xs, pre_norm[l][None, :], w_in_p, l, b_f_p[l][None, :], tri, tm)
        fox = _fox_call(qp, kp, vp, bq, bk, sub)
        ret = _ret_call(rq, rk, rv, tabs, tb)
        xs = _out_call(xs, fox, fz, ret, rz, mq, mz, mk, mv, l, w_out_b,
                       post_norm[l][None, :], tm_out)
    return xs[None]
```

```python
import functools

import jax
import jax.numpy as jnp
from jax import lax
from jax.experimental import pallas as pl
from jax.experimental.pallas import tpu as pltpu

F32 = jnp.float32
BF16 = jnp.bfloat16

D_MODEL = 1024
DEPTH = 4
N_MEM = 256
HEAD_DIM = 64
FOX_HEADS = 8
RET_HEADS = 4
MEM_HEADS = 4
FOX_W = FOX_HEADS * HEAD_DIM
RET_W = RET_HEADS * HEAD_DIM
MEM_W = MEM_HEADS * HEAD_DIM
D_MIX = FOX_W + RET_W + MEM_W
RET_CHUNK = 128
ROPE_BASE = 10000.0
EPS = 1e-6
QK_SCALE = HEAD_DIM ** -0.5

LANES = 128
SUBLANES = 8
LOG2E = 1.4426950408889634
NEG_BIG = -1e30

OFF_FQ = 0
OFF_FK = OFF_FQ + FOX_W
OFF_FV = OFF_FK + FOX_W
OFF_FZ = OFF_FV + FOX_W
OFF_RQ = OFF_FZ + FOX_W
OFF_RK = OFF_RQ + RET_W
OFF_RV = OFF_RK + RET_W
OFF_RZ = OFF_RV + RET_W
OFF_MQ = OFF_RZ + RET_W
OFF_MZ = OFF_MQ + MEM_W
OFF_FF = OFF_MZ + MEM_W

X_LANE = HEAD_DIM
VT_ROWS = 80

VMEM_LIMIT = 56 * 1024 * 1024


def _split3(x):
    hi = x.astype(BF16).astype(F32)
    r1 = x - hi
    mid = r1.astype(BF16).astype(F32)
    lo = (r1 - mid).astype(BF16).astype(F32)
    return hi, mid, lo


def _log_sigmoid(x):
    return -(jnp.maximum(-x, 0.0) + jnp.log1p(jnp.exp(-jnp.abs(x))))


def _silu(z):
    return z * (1.0 / (1.0 + jnp.exp(-z)))


def _proj_kernel(x_ref, g_ref, wh_ref, wt_ref, wf_ref, bf_ref, tri_ref,
                 qp_ref, kp_ref, vp_ref, fz_ref, rq_ref, rk_ref, rv_ref, rz_ref,
                 mq_ref, mz_ref, carry_ref):
    i = pl.program_id(0)

    @pl.when(i == 0)
    def _():
        carry_ref[...] = jnp.zeros_like(carry_ref)

    x = x_ref[...]
    ms = jnp.mean(x * x, axis=-1, keepdims=True)
    h = (x * lax.rsqrt(ms + EPS) * g_ref[...]).astype(BF16)

    def proj(off, width):
        if off < OFF_RQ:
            w = wh_ref[0, :, off:off + width]
        elif off < OFF_FF:
            w = wt_ref[0, :, off - OFF_RQ:off - OFF_RQ + width]
        else:
            w = wf_ref[0]
        return jnp.dot(h, w, preferred_element_type=F32)

    tm = x.shape[0]
    lane = lax.broadcasted_iota(jnp.int32, (tm, LANES), 1)

    lf = _log_sigmoid(proj(OFF_FF, LANES) + bf_ref[...])
    fq = proj(OFF_FQ, FOX_W) * (QK_SCALE * LOG2E)
    fk = proj(OFF_FK, FOX_W)
    fv = proj(OFF_FV, FOX_W)
    tri = tri_ref[...]
    c = carry_ref[...]
    for part in _split3(lf):
        c = c + jnp.dot(tri, part.astype(BF16), preferred_element_type=F32)
    carry_ref[...] = c[tm - 1:tm, :]
    c_parts = _split3(c * LOG2E)

    for hd in range(FOX_HEADS):
        lo = LANES * (hd // 2)

        def head_chunk(a):
            ch = a[:, lo:lo + LANES]
            if hd % 2:
                ch = pltpu.roll(ch, HEAD_DIM, 1)
            return jnp.where(lane < HEAD_DIM, ch, 0.0)

        b0, b1, b2 = (p[:, hd:hd + 1] for p in c_parts)
        q = head_chunk(fq)
        q = jnp.where(lane == X_LANE, b0, q)
        q = jnp.where(lane == X_LANE + 1, b1, q)
        q = jnp.where(lane == X_LANE + 2, b2, q)
        q = jnp.where((lane >= X_LANE + 3) & (lane < X_LANE + 6), 1.0, q)
        k = head_chunk(fk)
        k = jnp.where((lane >= X_LANE) & (lane < X_LANE + 3), 1.0, k)
        k = jnp.where(lane == X_LANE + 3, -b0, k)
        k = jnp.where(lane == X_LANE + 4, -b1, k)
        k = jnp.where(lane == X_LANE + 5, -b2, k)
        v = jnp.where(lane == X_LANE, 1.0, head_chunk(fv))
        qp_ref[hd] = q.T.astype(BF16)
        kp_ref[hd] = k.astype(BF16)
        vp_ref[hd] = v.T[:VT_ROWS, :].astype(BF16)

    fz_ref[...] = proj(OFF_FZ, FOX_W)
    rq_ref[...] = proj(OFF_RQ, RET_W)
    rk_ref[...] = proj(OFF_RK, RET_W)
    rv_ref[...] = proj(OFF_RV, RET_W).astype(BF16)
    rz_ref[...] = proj(OFF_RZ, RET_W)
    mq_ref[...] = (proj(OFF_MQ, MEM_W) * QK_SCALE).astype(BF16)
    mz_ref[...] = proj(OFF_MZ, MEM_W)


def _proj_call(x, g_pre, w_parts, layer, b_f_p, tri, tm):
    S = x.shape[0]
    w_spec = lambda w: pl.BlockSpec((1, D_MODEL, min(w.shape[2], OFF_RQ)), lambda i: (layer, 0, 0))
    row = lambda w: pl.BlockSpec((tm, w), lambda i: (i, 0))
    const = lambda shape: pl.BlockSpec(shape, lambda i: (0,) * len(shape))
    head = pl.BlockSpec((FOX_HEADS, tm, LANES), lambda i: (0, i, 0))
    head_t = pl.BlockSpec((FOX_HEADS, LANES, tm), lambda i: (0, 0, i))
    head_vt = pl.BlockSpec((FOX_HEADS, VT_ROWS, tm), lambda i: (0, 0, i))
    out_shape = (
        jax.ShapeDtypeStruct((FOX_HEADS, LANES, S), BF16),
        jax.ShapeDtypeStruct((FOX_HEADS, S, LANES), BF16),
        jax.ShapeDtypeStruct((FOX_HEADS, VT_ROWS, S), BF16),
        jax.ShapeDtypeStruct((S, FOX_W), F32),
        jax.ShapeDtypeStruct((S, RET_W), F32),
        jax.ShapeDtypeStruct((S, RET_W), F32),
        jax.ShapeDtypeStruct((S, RET_W), BF16),
        jax.ShapeDtypeStruct((S, RET_W), F32),
        jax.ShapeDtypeStruct((S, MEM_W), BF16),
        jax.ShapeDtypeStruct((S, MEM_W), F32),
    )
    return pl.pallas_call(
        _proj_kernel,
        grid=(S // tm,),
        in_specs=[row(D_MODEL), const((1, D_MODEL))] + [w_spec(w) for w in w_parts]
                 + [const((1, LANES)), const((tm, tm))],
        out_specs=(head_t, head, head_vt, row(FOX_W), row(RET_W), row(RET_W), row(RET_W),
                   row(RET_W), row(MEM_W), row(MEM_W)),
        out_shape=out_shape,
        scratch_shapes=[pltpu.VMEM((1, LANES), F32)],
        compiler_params=pltpu.CompilerParams(
            dimension_semantics=("arbitrary",), vmem_limit_bytes=VMEM_LIMIT),
        name="proj",
    )(x, g_pre, *w_parts, b_f_p, tri)


def _max_rows(s):
    rows = s.shape[0]
    while rows % (2 * SUBLANES) == 0:
        rows //= 2
        s = jnp.maximum(s[:rows, :], s[rows:, :])
    return jnp.max(s, axis=0, keepdims=True)


def _fox_kernel(qt_ref, qtn_ref, kblk_ref, vtblk_ref, o_ref, m_ref, acc_ref, sa_ref, sb_ref, sc_ref,
                k_ref, vt_ref, *, bq, bk, sub):
    assert bk == bq
    i = pl.program_id(1)
    nsub = bq // sub
    chains = [(hh, sb) for hh in range(2) for sb in range(nsub)]

    m_ref[...] = jnp.full(m_ref.shape, NEG_BIG, F32)
    acc_ref[...] = jnp.zeros(acc_ref.shape, F32)

    new = pl.ds(pl.multiple_of(i * bk, bk), bk)
    k_ref[:, new, :] = kblk_ref[...]
    vt_ref[:, :, new] = vtblk_ref[...]

    def scores(c, q_ref, kb, nk=bk):
        hh, sb = c
        k = k_ref[hh, pl.ds(pl.multiple_of(kb * bk, bk), nk), :]
        return jnp.dot(k, q_ref[hh, :, sb * sub:(sb + 1) * sub], preferred_element_type=F32)

    def finish(c, s_ref, kb, masked):
        hh, sb = c
        cols = slice(sb * sub, (sb + 1) * sub)
        nk = (sb + 1) * sub if masked else bk
        s = s_ref[:nk, :]
        if masked:
            key = lax.broadcasted_iota(jnp.int32, (nk, sub), 0)
            qry = lax.broadcasted_iota(jnp.int32, (nk, sub), 1) + sb * sub
            s = jnp.where(key <= qry, s, NEG_BIG)
        vt = vt_ref[hh, :, pl.ds(pl.multiple_of(kb * bk, bk), nk)]
        m = m_ref[hh, :, cols]
        m_new = jnp.maximum(m, _max_rows(s))
        p = jnp.exp2(s - m_new)
        alpha = jnp.exp2(m - m_new)
        pv = jnp.dot(vt, p.astype(BF16), preferred_element_type=F32)
        acc_ref[hh, :, cols] = alpha * acc_ref[hh, :, cols] + pv
        m_ref[hh, :, cols] = m_new

    def step(kb, cur, nxt, next_is_diagonal=False):
        for n, c in enumerate(chains):
            nk = (c[1] + 1) * sub if next_is_diagonal else bk
            nxt[n, :nk, :] = scores(c, qt_ref, kb + 1, nk)
            finish(c, cur.at[n], kb, False)

    def last_step(cur, nxt):
        for n, c in enumerate(chains):
            nxt[n] = scores(c, qtn_ref, 0)
            finish(c, cur.at[n], i, True)

    @pl.when(i == 0)
    def _():
        for n, c in enumerate(chains):
            sa_ref[n] = scores(c, qt_ref, 0)

    def block(even_buf, odd_buf, handover, odd_block):
        def body(j, carry):
            step(2 * j, even_buf, odd_buf)
            step(2 * j + 1, odd_buf, even_buf)
            return carry

        lax.fori_loop(0, i // 2, body, 0)
        if odd_block:
            step(i - 1, even_buf, odd_buf, next_is_diagonal=True)
            last_step(odd_buf, handover)
        else:
            last_step(even_buf, handover)

    @pl.when(i % 2 == 0)
    def _():
        block(sa_ref, sb_ref, sc_ref, False)

    @pl.when(i % 2 == 1)
    def _():
        block(sc_ref, sb_ref, sa_ref, True)

    outs = [acc_ref[hh, :HEAD_DIM, :] / acc_ref[hh, X_LANE:X_LANE + 1, :] for hh in range(2)]
    o_ref[...] = jnp.concatenate(outs, axis=0).T


def _fox_call(qt, kp, vt, bq, bk, sub):
    _, S, _ = kp.shape
    nq = S // bq
    score_buf = pltpu.VMEM((2 * bq // sub, bk, sub), F32)
    return pl.pallas_call(
        functools.partial(_fox_kernel, bq=bq, bk=bk, sub=sub),
        grid=(FOX_HEADS // 2, S // bq),
        in_specs=[pl.BlockSpec((2, LANES, bq), lambda p, i: (p, 0, i)),
                  pl.BlockSpec((2, LANES, bq), lambda p, i: (p, 0, jnp.minimum(i + 1, nq - 1))),
                  pl.BlockSpec((2, bk, LANES), lambda p, i: (p, i, 0)),
                  pl.BlockSpec((2, VT_ROWS, bk), lambda p, i: (p, 0, i))],
        out_specs=pl.BlockSpec((bq, LANES), lambda p, i: (i, p)),
        out_shape=jax.ShapeDtypeStruct((S, FOX_W), F32),
        scratch_shapes=[pltpu.VMEM((2, 1, bq), F32), pltpu.VMEM((2, VT_ROWS, bq), F32),
                        score_buf, score_buf, score_buf,
                        pltpu.VMEM((2, S, LANES), BF16), pltpu.VMEM((2, VT_ROWS, S), BF16)],
        compiler_params=pltpu.CompilerParams(
            dimension_semantics=("arbitrary", "arbitrary"), vmem_limit_bytes=VMEM_LIMIT),
        name="fox",
    )(qt, qt, kp, vt)


def _ret_kernel(q_ref, k_ref, v_ref, cos_ref, sin_ref, dintra_ref, xi_ref, zeta_ref, g_ref,
                o_ref, r_ref, *, tb):
    j = pl.program_id(1)

    @pl.when(j == 0)
    def _():
        r_ref[...] = jnp.zeros_like(r_ref)

    C = RET_CHUNK
    half = HEAD_DIM // 2
    lane = lax.broadcasted_iota(jnp.int32, (tb, LANES), 1)
    first_half = (lane % HEAD_DIM) < half
    cos = cos_ref[...]
    sin = sin_ref[...]

    def rotary(x):
        swapped = jnp.where(first_half, pltpu.roll(x, LANES - half, 1), pltpu.roll(x, half, 1))
        return x * cos + swapped * sin

    q = rotary(q_ref[...])
    k = rotary(k_ref[...]) * QK_SCALE
    v = v_ref[...]
    xi = xi_ref[0]
    zeta = zeta_ref[0]
    g = g_ref[0]
    lane_c = lax.broadcasted_iota(jnp.int32, (C, LANES), 1)
    head0 = lane_c < HEAD_DIM
    same_head = (lax.broadcasted_iota(jnp.int32, (LANES, LANES), 0) // HEAD_DIM
                 == lax.broadcasted_iota(jnp.int32, (LANES, LANES), 1) // HEAD_DIM)
    nt = (((1,), (1,)), ((), ()))
    tn = (((0,), (0,)), ((), ()))

    chunks = range(tb // C)
    qc = [q[n * C:(n + 1) * C, :] for n in chunks]
    kc = [k[n * C:(n + 1) * C, :] for n in chunks]
    vc = [v[n * C:(n + 1) * C, :] for n in chunks]
    a = [[lax.dot_general(jnp.where(head0 if hh == 0 else ~head0, qc[n], 0.0).astype(BF16),
                          kc[n].astype(BF16), nt, preferred_element_type=F32)
          for hh in range(2)] for n in chunks]
    kv = [lax.dot_general((kc[n] * zeta).astype(BF16), vc[n], tn, preferred_element_type=F32)
          for n in chunks]
    o_intra = [[jnp.dot((a[n][hh] * dintra_ref[0, hh]).astype(BF16), vc[n],
                        preferred_element_type=F32)
                for hh in range(2)] for n in chunks]
    R = r_ref[...]
    for n in chunks:
        o = jnp.where(head0, o_intra[n][0], o_intra[n][1])
        o = o + jnp.dot((qc[n] * xi).astype(BF16), R.astype(BF16), preferred_element_type=F32)
        R = g * R + jnp.where(same_head, kv[n], 0.0)
        sq = o * o
        s0 = jnp.sum(jnp.where(head0, sq, 0.0), axis=1, keepdims=True)
        s1 = jnp.sum(jnp.where(head0, 0.0, sq), axis=1, keepdims=True)
        ms = jnp.where(head0, s0, s1) * (1.0 / HEAD_DIM)
        o_ref[n * C:(n + 1) * C, :] = o * lax.rsqrt(ms + EPS)
    r_ref[...] = R


def _ret_call(rq, rk, rv, tabs, tb):
    S = rq.shape[0]
    cos, sin, dintra, xi, zeta, g = tabs
    row = pl.BlockSpec((tb, LANES), lambda p, j: (j, p))
    tab = pl.BlockSpec((tb, LANES), lambda p, j: (j, 0))
    pair = lambda shape: pl.BlockSpec((1,) + shape, lambda p, j: (p,) + (0,) * len(shape))
    return pl.pallas_call(
        functools.partial(_ret_kernel, tb=tb),
        grid=(RET_HEADS // 2, S // tb),
        in_specs=[row, row, row, tab, tab,
                  pair((2, RET_CHUNK, RET_CHUNK)), pair((RET_CHUNK, LANES)),
                  pair((RET_CHUNK, LANES)), pair((LANES, LANES))],
        out_specs=row,
        out_shape=jax.ShapeDtypeStruct((S, RET_W), F32),
        scratch_shapes=[pltpu.VMEM((LANES, LANES), F32)],
        compiler_params=pltpu.CompilerParams(
            dimension_semantics=("arbitrary", "arbitrary"), vmem_limit_bytes=VMEM_LIMIT),
        name="ret",
    )(rq, rk, rv, cos, sin, dintra, xi, zeta, g)


def _ret_tables(S):
    half = HEAD_DIM // 2
    C = RET_CHUNK
    pos = jnp.arange(S, dtype=F32)
    freqs = ROPE_BASE ** (-jnp.arange(half, dtype=F32) / half)
    ang = pos[:, None] * freqs[None, :]
    cos, sin = jnp.cos(ang), jnp.sin(ang)
    cos_t = jnp.tile(jnp.concatenate([cos, cos], axis=1), (1, 2))
    sin_t = jnp.tile(jnp.concatenate([-sin, sin], axis=1), (1, 2))
    log_gamma = jnp.log(1.0 - 2.0 ** (-5.0 - jnp.arange(RET_HEADS, dtype=F32)))
    idx = jnp.arange(C, dtype=F32)
    diff = idx[:, None] - idx[None, :]
    decay_intra = jnp.where(diff >= 0, jnp.exp(log_gamma[:, None, None] * jnp.maximum(diff, 0.0)), 0.0)
    zeta = jnp.exp(log_gamma[:, None] * (C - 1 - idx)[None, :])
    xi = jnp.exp(log_gamma[:, None] * (idx + 1)[None, :])
    decay_chunk = jnp.exp(log_gamma * C)
    npair = RET_HEADS // 2
    dintra = decay_intra.reshape(npair, 2, C, C)
    lanes = lambda t: jnp.repeat(t.reshape(npair, 2, C).transpose(0, 2, 1), HEAD_DIM, axis=2)
    gd = jnp.repeat(decay_chunk.reshape(npair, 2), HEAD_DIM, axis=1)
    blk = jnp.arange(LANES) // HEAD_DIM
    g = jnp.where(blk[:, None] == blk[None, :], gd[:, :, None], 0.0)
    return cos_t, sin_t, dintra, lanes(xi), lanes(zeta), g


def _memkv_kernel(mem_ref, g_ref, w_ref, mk_ref, mv_ref):
    x = mem_ref[...]
    ms = jnp.mean(x * x, axis=-1, keepdims=True)
    h = (x * lax.rsqrt(ms + EPS) * g_ref[0]).astype(BF16)
    kv = jnp.dot(h, w_ref[0], preferred_element_type=F32)
    mk_ref[0] = kv[:, :MEM_W].astype(BF16)
    mv_ref[0] = kv[:, MEM_W:].astype(BF16)


def _memkv_call(mem, g_mem, w_mkv):
    M = mem.shape[0]
    return pl.pallas_call(
        _memkv_kernel,
        grid=(DEPTH,),
        in_specs=[pl.BlockSpec((M, D_MODEL), lambda l: (0, 0)),
                  pl.BlockSpec((1, 1, D_MODEL), lambda l: (l, 0, 0)),
                  pl.BlockSpec((1, D_MODEL, 2 * MEM_W), lambda l: (l, 0, 0))],
        out_specs=(pl.BlockSpec((1, M, MEM_W), lambda l: (l, 0, 0)),
                   pl.BlockSpec((1, M, MEM_W), lambda l: (l, 0, 0))),
        out_shape=(jax.ShapeDtypeStruct((DEPTH, M, MEM_W), BF16),
                   jax.ShapeDtypeStruct((DEPTH, M, MEM_W), BF16)),
        compiler_params=pltpu.CompilerParams(
            dimension_semantics=("arbitrary",), vmem_limit_bytes=VMEM_LIMIT),
        name="memkv",
    )(mem, g_mem, w_mkv)


def _out_kernel(x_ref, fox_ref, fz_ref, ret_ref, rz_ref, mq_ref, mz_ref, mk_ref, mv_ref,
                w_ref, g_ref, o_ref):
    tm = x_ref.shape[0]
    lane = lax.broadcasted_iota(jnp.int32, (tm, LANES), 1)
    head0 = lane < HEAD_DIM
    nt = (((1,), (1,)), ((), ()))

    pairs = range(MEM_HEADS // 2)
    s = []
    for pr in pairs:
        cols = slice(pr * LANES, (pr + 1) * LANES)
        q2 = mq_ref[:, cols]
        for hh in range(2):
            qm = jnp.where(head0 if hh == 0 else ~head0, q2, jnp.zeros_like(q2))
            s.append(lax.dot_general(qm, mk_ref[0, :, cols], nt, preferred_element_type=F32))

    y_fr = jnp.concatenate([
        (fox_ref[...] * _silu(fz_ref[...])).astype(BF16),
        (ret_ref[...] * _silu(rz_ref[...])).astype(BF16),
    ], axis=1)
    o = jnp.dot(y_fr, w_ref[0, :FOX_W + RET_W, :], preferred_element_type=F32)

    mxa = []
    for pr in pairs:
        cols = slice(pr * LANES, (pr + 1) * LANES)
        o_h = []
        for hh in range(2):
            sc = s[2 * pr + hh]
            e = jnp.exp(sc - jnp.max(sc, axis=1, keepdims=True))
            p = e / jnp.sum(e, axis=1, keepdims=True)
            o_h.append(jnp.dot(p.astype(BF16), mv_ref[0, :, cols], preferred_element_type=F32))
        mxa.append(jnp.where(head0, o_h[0], o_h[1]))
    y_m = (jnp.concatenate(mxa, axis=1) * _silu(mz_ref[...])).astype(BF16)
    o = o + jnp.dot(y_m, w_ref[0, FOX_W + RET_W:, :], preferred_element_type=F32)
    ms = jnp.mean(o * o, axis=-1, keepdims=True)
    o_ref[...] = x_ref[...] + o * lax.rsqrt(ms + EPS) * g_ref[...]


def _out_call(x, fox, fz, ret, rz, mq, mz, mk, mv, layer, w_out, g_post, tm):
    S = x.shape[0]
    M = mk.shape[1]
    row = lambda w: pl.BlockSpec((tm, w), lambda i: (i, 0))
    const = lambda shape: pl.BlockSpec(shape, lambda i: (0,) * len(shape))
    mem_spec = pl.BlockSpec((1, M, MEM_W), lambda i: (layer, 0, 0))
    return pl.pallas_call(
        _out_kernel,
        grid=(S // tm,),
        in_specs=[row(D_MODEL), row(FOX_W), row(FOX_W), row(RET_W), row(RET_W), row(MEM_W),
                  row(MEM_W), mem_spec, mem_spec,
                  pl.BlockSpec((1, D_MIX, D_MODEL), lambda i: (layer, 0, 0)), const((1, D_MODEL))],
        out_specs=row(D_MODEL),
        out_shape=jax.ShapeDtypeStruct((S, D_MODEL), F32),
        compiler_params=pltpu.CompilerParams(
            dimension_semantics=("arbitrary",), vmem_limit_bytes=VMEM_LIMIT),
        name="out",
    )(x, fox, fz, ret, rz, mq, mz, mk, mv, w_out, g_post)


def _tiles(S):
    tm = min(512, S)
    tm_out = min(1024, S)
    bq = min(1024, S)
    bk = bq
    sub = min(256, bq)
    tb = min(2048, S)
    return tm, tm_out, bq, bk, sub, tb


def _split_w_in(w_in):
    ff0 = 4 * FOX_W
    w_bf = w_in.astype(BF16)
    ff = jnp.pad(w_bf[:, :, ff0:ff0 + FOX_HEADS], ((0, 0), (0, 0), (0, LANES - FOX_HEADS)))
    return w_bf, w_bf[:, :, ff0 + FOX_HEADS:], ff


def kernel(x, mem, w_in, b_f, w_out, w_mem_kv, pre_norm, post_norm, mem_norm):
    B, S, _ = x.shape
    assert B == 1
    tm, tm_out, bq, bk, sub, tb = _tiles(S)
    w_in_p = _split_w_in(w_in)
    b_f_p = jnp.pad(b_f, ((0, 0), (0, LANES - FOX_HEADS)))
    tri = jnp.tril(jnp.ones((tm, tm), F32)).astype(BF16)
    tabs = _ret_tables(S)
    mk, mv = _memkv_call(mem[0], mem_norm[:, None, :], w_mem_kv.astype(BF16))
    w_out_b = w_out.astype(BF16)

    xs = x[0]
    for l in range(DEPTH):
        qp, kp, vp, fz, rq, rk, rv, rz, mq, mz = _proj_call(
            xs, pre_norm[l][None, :], w_in_p, l, b_f_p[l][None, :], tri, tm)
        fox = _fox_call(qp, kp, vp, bq, bk, sub)
        ret = _ret_call(rq, rk, rv, tabs, tb)
        xs = _out_call(xs, fox, fz, ret, rz, mq, mz, mk, mv, l, w_out_b,
                       post_norm[l][None, :], tm_out)
    return xs[None]
```
